```python
import math
import jax, jax.numpy as jnp
from jax import lax
import numpy as np

D_MODEL = 1024
BATCH = 8
SEQ = 8192
DEPTH = 1
DEC_BATCH = 128
DEC_SEQ = 1
PAST_LEN = 8192
PAGE_SIZE = 128

HEAD_DIM = 64
D_MIX = D_MODEL
D_RWKV = D_MIX // 2
D_ATT = D_MIX - D_RWKV
N_RWKV_HEADS = D_RWKV // HEAD_DIM
N_ATT_HEADS = D_ATT // HEAD_DIM
LORA_W = 64
LORA_A = 64
LORA_G = 128
N_IDX_HEADS = 8
IDX_DIM = 64
TOPK_MAX = 256
ROPE_THETA = 500000.0
Q_BLOCK = 128
N_GROUPS = 4
EXPERTS_PER_GROUP = 8
N_EXPERTS = N_GROUPS * EXPERTS_PER_GROUP
TOPK_EXPERTS = 2
D_EXPERT = 256
MOE_BLOCK = 128
LN_EPS = 1e-5
GN_EPS = HEAD_DIM * 1e-5
DECAY_SCALE = math.exp(-0.5)
ALPHA = (2 * DEPTH) ** 0.25
BETA = (8 * DEPTH) ** -0.25
NEG = -1e30
RW_COLS = 3 * D_RWKV + LORA_W + LORA_A + LORA_G
ATT_COLS = 3 * D_ATT + N_IDX_HEADS * IDX_DIM + IDX_DIM + N_IDX_HEADS
N_IN = RW_COLS + ATT_COLS

kernel_name = 'hymba_rwkv7_dsa_hmoe_step'


def layer_norm(x, g, b, eps=LN_EPS):
    xf = x.astype(jnp.float32)
    mu = jnp.mean(xf, -1, keepdims=True)
    var = jnp.mean(jnp.square(xf - mu), -1, keepdims=True)
    return ((xf - mu) * lax.rsqrt(var + eps) * g + b).astype(x.dtype)


def rope_partial(x, pos):
    rd = x.shape[-1] // 4
    half = rd // 2
    inv = ROPE_THETA ** (-jnp.arange(half, dtype=jnp.float32) / half)
    ang = pos.astype(jnp.float32)[:, None] * inv[None, :]
    cos = jnp.cos(ang)[:, None, :]
    sin = jnp.sin(ang)[:, None, :]
    xf = x.astype(jnp.float32)
    x1, x2 = xf[..., :half], xf[..., half:rd]
    out = jnp.concatenate([x1 * cos - x2 * sin, x1 * sin + x2 * cos, xf[..., rd:]], -1)
    return out.astype(x.dtype)


def mixer_inputs(x, x_prev, pos, w_in, rw_mu, rw_w0, rw_w_up, rw_a0, rw_a_up, rw_g_up,
                 rw_k_k, rw_k_a, idx_ln_g, idx_ln_b):
    bn, t = x.shape[0], x.shape[1]
    p = jnp.einsum('btd,dc->btc', x, w_in)
    p_rw = p[..., :RW_COLS]
    p_first = jnp.einsum('bd,dc->bc', x_prev.astype(x.dtype), w_in[:, :RW_COLS])
    p_prev = jnp.concatenate([p_first[:, None], p_rw[:, :-1]], axis=1)
    m = (p_rw + rw_mu * (p_prev - p_rw)).astype(jnp.float32)
    o1, o2, o3 = D_RWKV, 2 * D_RWKV, 3 * D_RWKV
    o4 = o3 + LORA_W
    o5 = o4 + LORA_A
    r, k, v = m[..., :o1], m[..., o1:o2], m[..., o2:o3]
    dw, da, dg = m[..., o3:o4], m[..., o4:o5], m[..., o5:]
    decay = jnp.exp(-DECAY_SCALE * jax.nn.sigmoid(rw_w0 + jnp.tanh(dw) @ rw_w_up))
    a = jax.nn.sigmoid(rw_a0 + da @ rw_a_up)
    g = jax.nn.sigmoid(dg) @ rw_g_up
    hs = (bn, t, N_RWKV_HEADS, HEAD_DIM)
    kk = (k * rw_k_k).reshape(hs)
    kk = kk * lax.rsqrt(jnp.maximum(jnp.sum(kk * kk, -1, keepdims=True), 1e-24))
    k = k * (1.0 + (a - 1.0) * rw_k_a)
    rw = (r.reshape(hs), decay.reshape(hs), k.reshape(hs), v.reshape(hs), kk, a.reshape(hs), g)
    pa = p[..., RW_COLS:]
    c1, c2, c3 = D_ATT, 2 * D_ATT, 3 * D_ATT
    c4 = c3 + N_IDX_HEADS * IDX_DIM
    c5 = c4 + IDX_DIM
    ah = (bn, t, N_ATT_HEADS, HEAD_DIM)
    q = rope_partial(pa[..., :c1].reshape(ah), pos)
    ka = rope_partial(pa[..., c1:c2].reshape(ah), pos)
    va = pa[..., c2:c3].reshape(ah)
    qi = rope_partial(pa[..., c3:c4].reshape(bn, t, N_IDX_HEADS, IDX_DIM), pos)
    ki = layer_norm(pa[..., c4:c5], idx_ln_g, idx_ln_b)
    ki = rope_partial(ki[:, :, None, :], pos)[:, :, 0, :]
    wi = pa[..., c5:] * (N_IDX_HEADS ** -0.5 * IDX_DIM ** -0.5)
    return rw, (q, ka, va, qi, ki, wi)


def wkv_scan(r, w, k, v, kk, a, s0):
    def step(s, inp):
        r_t, w_t, k_t, v_t, kk_t, a_t = inp
        sa = jnp.einsum('bhvk,bhk->bhv', s, kk_t)
        s = (s * w_t[:, :, None, :] - sa[..., None] * (kk_t * a_t)[:, :, None, :]
             + v_t[..., None] * k_t[:, :, None, :])
        return s, jnp.einsum('bhvk,bhk->bhv', s, r_t)
    xs = tuple(jnp.moveaxis(z, 1, 0) for z in (r, w, k, v, kk, a))
    s_fin, out = lax.scan(step, s0, xs)
    return jnp.moveaxis(out, 0, 1), s_fin


def rwkv_mix(rw, s0, rw_r_k, rw_ln_g, rw_ln_b):
    r, decay, k, v, kk, a, g = rw
    o, s_fin = wkv_scan(r, decay, k, v, kk, a, s0.astype(jnp.float32))
    mu = jnp.mean(o, -1, keepdims=True)
    var = jnp.mean(jnp.square(o - mu), -1, keepdims=True)
    o = (o - mu) * lax.rsqrt(var + GN_EPS)
    bn, t = o.shape[0], o.shape[1]
    o = o.reshape(bn, t, D_RWKV) * rw_ln_g + rw_ln_b
    bonus = jnp.sum(r * k * rw_r_k, -1, keepdims=True) * v
    o = (o + bonus.reshape(bn, t, D_RWKV)) * g
    return o, s_fin


def dsa_prompt(q, k, v, qi, ki, wi):
    bn, t = q.shape[0], q.shape[1]
    ktop = min(TOPK_MAX, t // 4)
    nb = t // Q_BLOCK
    def to_blocks(z):
        return z.reshape((bn * nb, Q_BLOCK) + z.shape[2:])
    b_ids = jnp.repeat(jnp.arange(bn), nb)
    t0s = jnp.tile(jnp.arange(nb) * Q_BLOCK, bn)
    key_pos = jnp.arange(t)
    def one_block(args):
        qb, qib, wib, b, t0 = args
        qpos = t0 + jnp.arange(Q_BLOCK)
        rel = jax.nn.relu(jnp.einsum('qhd,sd->qhs', qib, ki[b]).astype(jnp.float32))
        score = jnp.einsum('qhs,qh->qs', rel, wib.astype(jnp.float32))
        score = jnp.where(key_pos[None, :] <= qpos[:, None], score, NEG)
        _, idx = lax.top_k(score, ktop)
        ksel = k[b][idx]
        vsel = v[b][idx]
        logit = jnp.einsum('qhd,qkhd->qhk', qb, ksel).astype(jnp.float32) * HEAD_DIM ** -0.5
        valid = (idx <= qpos[:, None])[:, None, :]
        p = jax.nn.softmax(jnp.where(valid, logit, NEG), axis=-1)
        return jnp.einsum('qhk,qkhd->qhd', p.astype(vsel.dtype), vsel)
    out = lax.map(one_block, (to_blocks(q), to_blocks(qi), to_blocks(wi), b_ids, t0s))
    return out.reshape(bn, t, D_ATT)


def dsa_sample(q, k_new, v_new, qi, ki_new, wi, cache_k, cache_v, cache_kidx, page_table):
    bn, t = q.shape[0], q.shape[1]
    n_pages = PAST_LEN // PAGE_SIZE
    past = n_pages * PAGE_SIZE
    ktop = min(TOPK_MAX, (past + t) // 4)
    ki_past = cache_kidx[page_table].reshape(bn, past, IDX_DIM)
    ki_all = jnp.concatenate([ki_past, ki_new.astype(ki_past.dtype)], axis=1)
    rel = jax.nn.relu(jnp.einsum('bqhd,bsd->bqhs', qi, ki_all).astype(jnp.float32))
    score = jnp.einsum('bqhs,bqh->bqs', rel, wi.astype(jnp.float32))
    qpos = past + jnp.arange(t)
    key_pos = jnp.arange(past + t)
    score = jnp.where(key_pos[None, None, :] <= qpos[None, :, None], score, NEG)
    _, idx = lax.top_k(score, ktop)
    in_past = idx < past
    pi = jnp.minimum(idx, past - 1)
    phys = jnp.take_along_axis(page_table, (pi // PAGE_SIZE).reshape(bn, -1), axis=1).reshape(idx.shape)
    off = pi % PAGE_SIZE
    ni = jnp.clip(idx - past, 0, t - 1)
    bidx = jnp.arange(bn)[:, None, None]
    sel = in_past[..., None, None]
    ksel = jnp.where(sel, cache_k[phys, off], k_new[bidx, ni].astype(cache_k.dtype))
    vsel = jnp.where(sel, cache_v[phys, off], v_new[bidx, ni].astype(cache_v.dtype))
    logit = jnp.einsum('bqhd,bqkhd->bqhk', q, ksel).astype(jnp.float32) * HEAD_DIM ** -0.5
    valid = (idx <= qpos[None, :, None])[:, :, None, :]
    p = jax.nn.softmax(jnp.where(valid, logit, NEG), axis=-1)
    out = jnp.einsum('bqhk,bqkhd->bqhd', p.astype(vsel.dtype), vsel)
    return out.reshape(bn, t, D_ATT)


def hier_moe(x, moe_w_grp, moe_w_exp, moe_w1, moe_w3, moe_w2):
    n_tok = x.shape[0]
    xf = x.astype(jnp.float32)
    grp_prob = jax.nn.softmax(xf @ moe_w_grp.astype(jnp.float32), axis=-1)
    grp = jnp.argmax(grp_prob, axis=-1)
    p_grp = jnp.take_along_axis(grp_prob, grp[:, None], axis=1)
    exp_logit = (xf @ moe_w_exp.astype(jnp.float32)).reshape(n_tok, N_GROUPS, EXPERTS_PER_GROUP)
    in_grp = jnp.take_along_axis(exp_logit, grp[:, None, None], axis=1)[:, 0]
    top_val, top_idx = lax.top_k(in_grp, TOPK_EXPERTS)
    gates = jax.nn.softmax(top_val, axis=-1) * p_grp
    expert = grp[:, None] * EXPERTS_PER_GROUP + top_idx
    n_asg = n_tok * TOPK_EXPERTS
    e_flat = expert.reshape(n_asg)
    tok_flat = jnp.repeat(jnp.arange(n_tok, dtype=jnp.int32), TOPK_EXPERTS)
    g_flat = gates.reshape(n_asg)
    order = jnp.argsort(e_flat)
    e_sorted = e_flat[order]
    counts = jnp.zeros((N_EXPERTS,), jnp.int32).at[e_flat].add(1)
    starts = jnp.cumsum(counts) - counts
    padded = (counts + MOE_BLOCK - 1) // MOE_BLOCK * MOE_BLOCK
    pad_ends = jnp.cumsum(padded)
    pad_starts = pad_ends - padded
    dest = pad_starts[e_sorted] + (jnp.arange(n_asg) - starts[e_sorted])
    n_blocks = -(-n_asg // MOE_BLOCK) + N_EXPERTS
    n_rows = n_blocks * MOE_BLOCK
    tok_buf = jnp.zeros((n_rows,), jnp.int32).at[dest].set(tok_flat[order])
    gate_buf = jnp.zeros((n_rows,), jnp.float32).at[dest].set(g_flat[order])
    blk_start = jnp.arange(n_blocks) * MOE_BLOCK
    blk_expert = jnp.minimum(jnp.sum(blk_start[:, None] >= pad_ends[None, :], axis=1), N_EXPERTS - 1)
    def expert_block(args):
        toks, e = args
        xb = x[toks]
        hdn = jax.nn.silu(xb @ moe_w1[e]) * (xb @ moe_w3[e])
        return hdn @ moe_w2[e]
    out = lax.map(expert_block, (tok_buf.reshape(n_blocks, MOE_BLOCK), blk_expert))
    out = out.reshape(n_rows, D_MODEL) * gate_buf[:, None].astype(x.dtype)
    return jnp.zeros_like(x).at[tok_buf].add(out)


def post_block(x, o_rw, o_att, w_out, ln1_g, ln1_b, moe_w_grp, moe_w_exp, moe_w1, moe_w3, moe_w2, ln2_g, ln2_b):
    heads = jnp.concatenate([o_rw.astype(jnp.float32), o_att.astype(jnp.float32)], -1).astype(x.dtype)
    mix = jnp.einsum('btc,cd->btd', heads, w_out)
    h = layer_norm(ALPHA * x + mix, ln1_g, ln1_b)
    ff = hier_moe(h.reshape(-1, D_MODEL), moe_w_grp, moe_w_exp, moe_w1, moe_w3, moe_w2).reshape(h.shape)
    return layer_norm(ALPHA * h + ff, ln2_g, ln2_b)


def setup_inputs(seed: int = 0) -> dict:
    key = jax.random.key(seed)
    ks = jax.random.split(key, 40)
    f32 = jnp.float32
    n_pages = PAST_LEN // PAGE_SIZE
    n_used = DEC_BATCH * n_pages
    n_pool = n_used + max(1, n_used // 4)
    def nrm(i, shape, scale):
        return jax.random.normal(ks[i], shape, f32) * scale
    page_table = jax.random.permutation(ks[0], n_pool)[:n_used].reshape(DEC_BATCH, n_pages).astype(jnp.int32)
    col_scale = jnp.concatenate([
        jnp.ones((2 * D_RWKV,), f32), jnp.full((D_RWKV,), BETA, f32),
        jnp.ones((LORA_W + LORA_A + LORA_G + 2 * D_ATT,), f32), jnp.full((D_ATT,), BETA, f32),
        jnp.ones((N_IDX_HEADS * IDX_DIM + IDX_DIM + N_IDX_HEADS,), f32)])
    return {
        'x_prompt': nrm(1, (BATCH, SEQ, D_MODEL), 1.0),
        'x_sample': nrm(2, (DEC_BATCH, DEC_SEQ, D_MODEL), 1.0),
        'cache_k': nrm(3, (n_pool, PAGE_SIZE, N_ATT_HEADS, HEAD_DIM), 1.0),
        'cache_v': nrm(4, (n_pool, PAGE_SIZE, N_ATT_HEADS, HEAD_DIM), BETA),
        'cache_kidx': nrm(5, (n_pool, PAGE_SIZE, IDX_DIM), 1.0),
        'state_wkv': nrm(6, (DEC_BATCH, N_RWKV_HEADS, HEAD_DIM, HEAD_DIM), 0.5),
        'state_shift': nrm(7, (DEC_BATCH, D_MODEL), 1.0),
        'page_table': page_table,
        'w_in': nrm(8, (D_MODEL, N_IN), D_MODEL ** -0.5) * col_scale[None, :],
        'rw_mu': jax.random.uniform(ks[9], (RW_COLS,), f32, 0.2, 0.8),
        'rw_w0': nrm(10, (D_RWKV,), 0.5),
        'rw_w_up': nrm(11, (LORA_W, D_RWKV), 0.5 * LORA_W ** -0.5),
        'rw_a0': nrm(12, (D_RWKV,), 0.1),
        'rw_a_up': nrm(13, (LORA_A, D_RWKV), 0.5 * LORA_A ** -0.5),
        'rw_g_up': nrm(14, (LORA_G, D_RWKV), LORA_G ** -0.5),
        'rw_k_k': 0.85 + nrm(15, (D_RWKV,), 0.05),
        'rw_k_a': 1.0 + nrm(16, (D_RWKV,), 0.05),
        'rw_r_k': nrm(17, (N_RWKV_HEADS, HEAD_DIM), 0.1),
        'rw_ln_g': 1.0 + nrm(18, (D_RWKV,), 0.05),
        'rw_ln_b': nrm(19, (D_RWKV,), 0.02),
        'idx_ln_g': 1.0 + nrm(20, (IDX_DIM,), 0.05),
        'idx_ln_b': nrm(21, (IDX_DIM,), 0.02),
        'w_out': nrm(22, (D_MIX, D_MODEL), BETA * D_MIX ** -0.5),
        'ln1_g': 1.0 + nrm(23, (D_MODEL,), 0.05),
        'ln1_b': nrm(24, (D_MODEL,), 0.02),
        'moe_w_grp': nrm(25, (D_MODEL, N_GROUPS), D_MODEL ** -0.5),
        'moe_w_exp': nrm(26, (D_MODEL, N_EXPERTS), D_MODEL ** -0.5),
        'moe_w1': nrm(27, (N_EXPERTS, D_MODEL, D_EXPERT), D_MODEL ** -0.5),
        'moe_w3': nrm(28, (N_EXPERTS, D_MODEL, D_EXPERT), D_MODEL ** -0.5),
        'moe_w2': nrm(29, (N_EXPERTS, D_EXPERT, D_MODEL), BETA * D_EXPERT ** -0.5),
        'ln2_g': 1.0 + nrm(30, (D_MODEL,), 0.05),
        'ln2_b': nrm(31, (D_MODEL,), 0.02),
    }


def reference(x_prompt, x_sample, cache_k, cache_v, cache_kidx, state_wkv, state_shift, page_table,
              w_in, rw_mu, rw_w0, rw_w_up, rw_a0, rw_a_up, rw_g_up, rw_k_k, rw_k_a, rw_r_k,
              rw_ln_g, rw_ln_b, idx_ln_g, idx_ln_b, w_out, ln1_g, ln1_b,
              moe_w_grp, moe_w_exp, moe_w1, moe_w3, moe_w2, ln2_g, ln2_b):
    proj = (w_in, rw_mu, rw_w0, rw_w_up, rw_a0, rw_a_up, rw_g_up, rw_k_k, rw_k_a, idx_ln_g, idx_ln_b)
    ffn = (w_out, ln1_g, ln1_b, moe_w_grp, moe_w_exp, moe_w1, moe_w3, moe_w2, ln2_g, ln2_b)
    bp, tp = x_prompt.shape[0], x_prompt.shape[1]
    rw_p, att_p = mixer_inputs(x_prompt, jnp.zeros((bp, D_MODEL), x_prompt.dtype), jnp.arange(tp), *proj)
    s0_p = jnp.zeros((bp, N_RWKV_HEADS, HEAD_DIM, HEAD_DIM), jnp.float32)
    o_rw_p, wkv_p = rwkv_mix(rw_p, s0_p, rw_r_k, rw_ln_g, rw_ln_b)
    q_p, k_p, v_p, qi_p, ki_p, wi_p = att_p
    o_att_p = dsa_prompt(q_p, k_p, v_p, qi_p, ki_p, wi_p)
    y_prompt = post_block(x_prompt, o_rw_p, o_att_p, *ffn)
    ts = x_sample.shape[1]
    rw_s, att_s = mixer_inputs(x_sample, state_shift, PAST_LEN + jnp.arange(ts), *proj)
    o_rw_s, wkv_s = rwkv_mix(rw_s, state_wkv, rw_r_k, rw_ln_g, rw_ln_b)
    q_s, k_s, v_s, qi_s, ki_s, wi_s = att_s
    o_att_s = dsa_sample(q_s, k_s, v_s, qi_s, ki_s, wi_s, cache_k, cache_v, cache_kidx, page_table)
    y_sample = post_block(x_sample, o_rw_s, o_att_s, *ffn)
    return (y_prompt, y_sample,
            k_p.astype(cache_k.dtype), v_p.astype(cache_v.dtype), ki_p.astype(cache_kidx.dtype),
            wkv_p.astype(state_wkv.dtype), x_prompt[:, -1].astype(state_shift.dtype),
            k_s.astype(cache_k.dtype), v_s.astype(cache_v.dtype), ki_s.astype(cache_kidx.dtype),
            wkv_s.astype(state_wkv.dtype), x_sample[:, -1].astype(state_shift.dtype))
```

```python
import functools
import math

import jax
import jax.numpy as jnp
from jax import lax
from jax.experimental import pallas as pl
from jax.experimental.pallas import tpu as pltpu

F32 = jnp.float32
BF16 = jnp.bfloat16
I32 = jnp.int32

D_MODEL = 1024
HEAD_DIM = 64
D_RWKV = 512
D_ATT = 512
N_HEADS = 8
LORA_W = 64
LORA_A = 64
LORA_G = 128
N_IDX_HEADS = 8
IDX_DIM = 64
TOPK_MAX = 256
ROPE_THETA = 500000.0
PAGE_SIZE = 128
N_GROUPS = 4
EXPERTS_PER_GROUP = 8
N_EXPERTS = 32
D_EXPERT = 256
LN_EPS = 1e-5
GN_EPS = HEAD_DIM * 1e-5
DECAY_SCALE = math.exp(-0.5)
ALPHA = 2.0 ** 0.25
NEG = -1e30
RW_COLS = 3 * D_RWKV + LORA_W + LORA_A + LORA_G
WI_SCALE = N_IDX_HEADS ** -0.5 * IDX_DIM ** -0.5
QK_SCALE = HEAD_DIM ** -0.5
INT_MIN = -(2 ** 31)
INT_MAX = 2 ** 31 - 1

LANES = 128
VMEM_LIMIT = 56 * 1024 * 1024
SCAN_CHUNK = 64
EXPERT_CHUNK = 4


def _dot(a, b):
    return jnp.dot(a, b, preferred_element_type=F32)


def _dot_nt(a, b):
    return lax.dot_general(a, b, (((1,), (1,)), ((), ())), preferred_element_type=F32)


def _dot_hi(a, b):
    return jnp.dot(a, b, preferred_element_type=F32, precision=lax.Precision.HIGHEST)


def _dot_nt_hi(a, b):
    return lax.dot_general(a, b, (((1,), (1,)), ((), ())), preferred_element_type=F32,
                           precision=lax.Precision.HIGHEST)


def _dot_tn_hi(a, b):
    return lax.dot_general(a, b, (((0,), (0,)), ((), ())), preferred_element_type=F32,
                           precision=lax.Precision.HIGHEST)


def _params(*sem):
    return pltpu.CompilerParams(dimension_semantics=sem, vmem_limit_bytes=VMEM_LIMIT)


def _full(shape):
    nd = len(shape)
    return pl.BlockSpec(shape, lambda *_: (0,) * nd)


def _rope(x, cc, s1, s2):
    outs = []
    for c in range(x.shape[1] // LANES):
        xc = x[:, c * LANES:(c + 1) * LANES]
        outs.append(xc * cc + pltpu.roll(xc, LANES - 8, 1) * s1 + pltpu.roll(xc, 8, 1) * s2)
    return outs[0] if len(outs) == 1 else jnp.concatenate(outs, axis=1)


def _proj_body(x_ref, xs_ref, wrw_ref, watt_ref, wtail_ref, mu_ref, w0_ref, wup_ref, a0_ref, aup_ref,
               gup_ref, kk_ref, ka_ref, lng_ref, lnb_ref, cc_ref, s1_ref, s2_ref,
               r_o, lw_o, k_o, v_o, kkr_o, a_o, g_o, qb_o, kf_o, vf_o, kb_o, vb_o, qib_o, kif_o, kib_o, wi_o):
    xb = x_ref[...].astype(BF16)
    xsb = xs_ref[...].astype(BF16)

    def lerp(c0, c1):
        w = wrw_ref[:, c0:c1]
        cur = _dot(xb, w)
        prev = _dot(xsb, w)
        return cur + mu_ref[:, c0:c1] * (prev - cur)

    r_o[...] = lerp(0, D_RWKV)
    k = lerp(D_RWKV, 2 * D_RWKV)
    v_o[...] = lerp(2 * D_RWKV, 3 * D_RWKV)
    lora = lerp(3 * D_RWKV, 3 * D_RWKV + LORA_W + LORA_A)
    dg = lerp(3 * D_RWKV + LORA_W + LORA_A, RW_COLS)
    lw_o[...] = -DECAY_SCALE * jax.nn.sigmoid(w0_ref[...] + _dot(jnp.tanh(lora).astype(BF16), wup_ref[...]))
    a = jax.nn.sigmoid(a0_ref[...] + _dot(lora.astype(BF16), aup_ref[...]))
    a_o[...] = a
    g_o[...] = _dot(jax.nn.sigmoid(dg).astype(BF16), gup_ref[...])
    kkr_o[...] = k * kk_ref[...]
    k_o[...] = k * (1.0 + (a - 1.0) * ka_ref[...])

    cc, s1, s2 = cc_ref[...], s1_ref[...], s2_ref[...]
    q = _rope(_dot(xb, watt_ref[:, 0:D_ATT]), cc, s1, s2)
    qb_o[...] = (q * QK_SCALE).astype(BF16)
    ka = _rope(_dot(xb, watt_ref[:, D_ATT:2 * D_ATT]), cc, s1, s2)
    kf_o[...] = ka
    kb_o[...] = ka.astype(BF16)
    va = _dot(xb, watt_ref[:, 2 * D_ATT:3 * D_ATT])
    vf_o[...] = va
    vb_o[...] = va.astype(BF16)
    qi = _rope(_dot(xb, watt_ref[:, 3 * D_ATT:4 * D_ATT]), cc, s1, s2)
    qib_o[...] = qi.astype(BF16)

    tail = _dot(xb, wtail_ref[...])
    t0 = tail[:, :LANES]
    in_ki = lax.broadcasted_iota(I32, (1, LANES), 1) < IDX_DIM
    mean = jnp.sum(jnp.where(in_ki, t0, 0.0), axis=1, keepdims=True) * (1.0 / IDX_DIM)
    d = jnp.where(in_ki, t0 - mean, 0.0)
    var = jnp.sum(d * d, axis=1, keepdims=True) * (1.0 / IDX_DIM)
    ki = _rope(d * lax.rsqrt(var + LN_EPS) * lng_ref[...] + lnb_ref[...], cc, s1, s2)[:, :IDX_DIM]
    kif_o[...] = ki
    kib_o[...] = ki.astype(BF16)
    wi_o[...] = tail[:, LANES:LANES + N_IDX_HEADS] * WI_SCALE


def _rope_tables(pos):
    half = HEAD_DIM // 8
    inv = ROPE_THETA ** (-jnp.arange(half, dtype=F32) / half)
    ang = pos.astype(F32)[:, None] * inv[None, :]
    cos, sin = jnp.cos(ang), jnp.sin(ang)
    n = pos.shape[0]
    one = jnp.ones((n, HEAD_DIM - 2 * half), F32)
    zero = jnp.zeros((n, HEAD_DIM - 2 * half), F32)
    zh = jnp.zeros((n, half), F32)
    cc = jnp.concatenate([cos, cos, one], 1)
    s1 = jnp.concatenate([-sin, zh, zero], 1)
    s2 = jnp.concatenate([zh, sin, zero], 1)
    return tuple(jnp.tile(t, (1, LANES // HEAD_DIM)) for t in (cc, s1, s2))


def _proj(x, xs, tables, pw, tm):
    n = x.shape[0]
    nt = tables[0].shape[0] // tm
    row = lambda w: pl.BlockSpec((tm, w), lambda i: (i, 0))
    tab = pl.BlockSpec((tm, LANES), lambda i: (i % nt, 0))
    weights = (pw["wrw"], pw["watt"], pw["wtail"], pw["mu"], pw["w0"], pw["wup"], pw["a0"], pw["aup"],
               pw["gup"], pw["kk"], pw["ka"], pw["lng"], pw["lnb"])
    f = lambda w, dt=F32: jax.ShapeDtypeStruct((n, w), dt)
    out_shape = ([f(D_RWKV)] * 7 + [f(D_ATT, BF16), f(D_ATT), f(D_ATT), f(D_ATT, BF16), f(D_ATT, BF16),
                                    f(D_ATT, BF16), f(IDX_DIM), f(IDX_DIM, BF16), f(N_IDX_HEADS)])
    out_specs = [row(D_RWKV)] * 7 + [row(D_ATT)] * 6 + [row(IDX_DIM), row(IDX_DIM), row(N_IDX_HEADS)]
    return pl.pallas_call(
        _proj_body,
        grid=(n // tm,),
        in_specs=[row(D_MODEL), row(D_MODEL)] + [_full(w.shape) for w in weights] + [tab] * 3,
        out_specs=out_specs,
        out_shape=out_shape,
        compiler_params=_params("parallel"),
        name="proj",
    )(x, xs, *weights, *tables)


def _head_epilogue(o, r, k, v, g, rk, lng, lnb):
    mu = jnp.mean(o, axis=-1, keepdims=True)
    d = o - mu
    var = jnp.mean(d * d, axis=-1, keepdims=True)
    on = d * lax.rsqrt(var + GN_EPS) * lng + lnb
    bonus = jnp.sum(r * k * rk, axis=-1, keepdims=True) * v
    return (on + bonus) * g


def _scan_chunk_body(r_ref, lw_ref, k_ref, v_ref, kk_ref, a_ref, g_ref, rk_ref, lng_ref, lnb_ref, s0_ref,
                     o_ref, sf_ref, s_scr):
    c = pl.program_id(1)

    @pl.when(c == 0)
    def _():
        s_scr[...] = s0_ref[0]

    C = r_ref.shape[0]
    row = lax.broadcasted_iota(I32, (C, C), 0)
    col = lax.broadcasted_iota(I32, (C, C), 1)
    strict = col < row
    incl = col <= row
    tri = incl.astype(F32)
    n_double = C.bit_length() - 1
    for h in range(N_HEADS):
        sl = slice(h * HEAD_DIM, (h + 1) * HEAD_DIM)
        r, lw, k, v = r_ref[:, sl], lw_ref[:, sl], k_ref[:, sl], v_ref[:, sl]
        kkr, a = kk_ref[:, sl], a_ref[:, sl]
        kk = kkr * lax.rsqrt(jnp.maximum(jnp.sum(kkr * kkr, axis=-1, keepdims=True), 1e-24))
        beta = kk * a
        cum = _dot_hi(tri, lw)
        cum_end = cum[C - 1:C, :]
        e_neg = jnp.exp(-cum)
        e_end = jnp.exp(cum_end - cum)
        at = -kk * jnp.exp(cum - lw)
        bt = beta * e_neg
        kt = k * e_neg
        rt = r * jnp.exp(cum)
        s0 = s_scr[h]
        n_mat = jnp.where(strict, _dot_nt_hi(at, bt), 0.0)
        a_ak = jnp.where(strict, _dot_nt_hi(at, kt), 0.0)
        a_rb = jnp.where(incl, _dot_nt_hi(rt, bt), 0.0)
        a_rk = jnp.where(incl, _dot_nt_hi(rt, kt), 0.0)
        u = _dot_nt_hi(at, s0) + _dot_hi(a_ak, v)
        p = n_mat
        for i in range(n_double):
            u = u + _dot_hi(p, u)
            if i + 1 < n_double:
                p = _dot_hi(p, p)
        o = _dot_nt_hi(rt, s0) + _dot_hi(a_rb, u) + _dot_hi(a_rk, v)
        s_scr[h] = s0 * jnp.exp(cum_end) + _dot_tn_hi(u, beta * e_end) + _dot_tn_hi(v, k * e_end)
        o_ref[:, sl] = _head_epilogue(o, r, k, v, g_ref[:, sl], rk_ref[:, sl], lng_ref[:, sl], lnb_ref[:, sl])

    @pl.when(c == pl.num_programs(1) - 1)
    def _():
        sf_ref[0] = s_scr[...]


def _scan_chunked(rw, s0, rk, lng, lnb, n_seq, t):
    nc = t // SCAN_CHUNK
    row = pl.BlockSpec((SCAN_CHUNK, D_RWKV), lambda b, c: (b * nc + c, 0))
    st = pl.BlockSpec((1, N_HEADS, HEAD_DIM, HEAD_DIM), lambda b, c: (b, 0, 0, 0))
    par = _full((1, D_RWKV))
    return pl.pallas_call(
        _scan_chunk_body,
        grid=(n_seq, nc),
        in_specs=[row] * 7 + [par] * 3 + [st],
        out_specs=[row, st],
        out_shape=[jax.ShapeDtypeStruct((n_seq * t, D_RWKV), F32),
                   jax.ShapeDtypeStruct((n_seq, N_HEADS, HEAD_DIM, HEAD_DIM), F32)],
        scratch_shapes=[pltpu.VMEM((N_HEADS, HEAD_DIM, HEAD_DIM), F32)],
        compiler_params=_params("parallel", "arbitrary"),
        name="scan_chunked",
    )(*rw, rk, lng, lnb, s0)


def _scan_step_body(r_ref, lw_ref, k_ref, v_ref, kk_ref, a_ref, g_ref, rk_ref, lng_ref, lnb_ref, s0_ref,
                    o_ref, sf_ref):
    eye = (lax.broadcasted_iota(I32, (HEAD_DIM, HEAD_DIM), 0)
           == lax.broadcasted_iota(I32, (HEAD_DIM, HEAD_DIM), 1)).astype(F32)
    outs = []
    for h in range(N_HEADS):
        sl = slice(h * HEAD_DIM, (h + 1) * HEAD_DIM)
        r, lw, k, v = r_ref[0, :, sl], lw_ref[0, :, sl], k_ref[0, :, sl], v_ref[0, :, sl]
        kkr, a = kk_ref[0, :, sl], a_ref[0, :, sl]
        kk = kkr * lax.rsqrt(jnp.maximum(jnp.sum(kkr * kkr, axis=-1, keepdims=True), 1e-24))
        s = s0_ref[0, h]
        sa = jnp.sum(s * kk, axis=1, keepdims=True)
        v_col = jnp.sum(eye * v, axis=1, keepdims=True)
        s = s * jnp.exp(lw) - sa * (kk * a) + v_col * k
        sf_ref[0, h] = s
        o_col = jnp.sum(s * r, axis=1, keepdims=True)
        o = jnp.sum(eye * o_col, axis=0, keepdims=True)
        outs.append(_head_epilogue(o, r, k, v, g_ref[0, :, sl], rk_ref[:, sl], lng_ref[:, sl], lnb_ref[:, sl]))
    o_ref[0] = jnp.concatenate(outs, axis=1)


def _scan_step(rw, s0, rk, lng, lnb, n_seq):
    row = pl.BlockSpec((1, 1, D_RWKV), lambda b: (b, 0, 0))
    st = pl.BlockSpec((1, N_HEADS, HEAD_DIM, HEAD_DIM), lambda b: (b, 0, 0, 0))
    par = _full((1, D_RWKV))
    rw3 = [z.reshape(n_seq, 1, D_RWKV) for z in rw]
    o, sf = pl.pallas_call(
        _scan_step_body,
        grid=(n_seq,),
        in_specs=[row] * 7 + [par] * 3 + [st],
        out_specs=[row, st],
        out_shape=[jax.ShapeDtypeStruct((n_seq, 1, D_RWKV), F32),
                   jax.ShapeDtypeStruct((n_seq, N_HEADS, HEAD_DIM, HEAD_DIM), F32)],
        compiler_params=_params("parallel"),
        name="scan_step",
    )(*rw3, rk, lng, lnb, s0)
    return o.reshape(n_seq, D_RWKV), sf


def _order_key(score):
    bits = pltpu.bitcast(score, I32)
    return bits ^ ((bits >> 31) & INT_MAX)


def _select_threshold(count_ge, count_tie_le, n_valid, ktop, rows, idx_bits):
    lo = jnp.full((rows, 1), INT_MIN + 1, I32)
    hi = jnp.where(n_valid <= ktop, lo + 1, INT_MAX)

    def cond(st):
        it, lo, hi = st
        return jnp.logical_and(it < 33, jnp.max(jnp.where(lo + 1 != hi, 1.0, 0.0)) > 0.0)

    def body(st):
        it, lo, hi = st
        mid = (lo & hi) + ((lo ^ hi) >> 1)
        cnt = count_ge(mid)
        ge = cnt >= ktop
        lo2 = jnp.where(ge, mid, lo)
        hi2 = jnp.where(cnt == ktop, mid + 1, jnp.where(ge, hi, mid))
        return it + 1, lo2, hi2

    _, thr, _ = lax.while_loop(cond, body, (jnp.int32(0), lo, hi))
    excess = jnp.logical_and(count_ge(thr) > ktop, n_valid > ktop)
    all_idx = jnp.full((rows, 1), (1 << idx_bits) - 1, I32)

    def tie_search():
        need = ktop - count_ge(thr + 1)

        def tbody(_, st):
            lo_j, hi_j = st
            mid = (lo_j + hi_j) >> 1
            ok = count_tie_le(thr, mid) >= need
            return jnp.where(ok, lo_j, mid + 1), jnp.where(ok, mid, hi_j)

        lo_j, _ = lax.fori_loop(0, idx_bits, tbody, (jnp.zeros((rows, 1), I32), all_idx))
        return jnp.where(excess, lo_j, all_idx)

    tie_end = lax.cond(jnp.max(jnp.where(excess, 1.0, 0.0)) > 0.0, tie_search, lambda: all_idx)
    return thr, tie_end


def _dsa_prompt_body(qi_ref, wi_ref, q_ref, ki_ref, k_ref, v_ref, o_ref, key_scr, *, ktop, tq, tk, idx_bits):
    i = pl.program_id(1)
    n_kt = (i + 1) * (tq // tk)
    qpos = i * tq + lax.broadcasted_iota(I32, (tq, 1), 0)
    lane_pos = lax.broadcasted_iota(I32, (1, tk), 1)
    qi = qi_ref[...]
    wi = wi_ref[...]

    def score_tile(kt, carry):
        kis = ki_ref[pl.ds(pl.multiple_of(kt * tk, tk), tk), :]
        acc = jnp.zeros((tq, tk), F32)
        for h in range(N_IDX_HEADS):
            s = _dot_nt(qi[:, h * IDX_DIM:(h + 1) * IDX_DIM], kis)
            acc = acc + jnp.maximum(s, 0.0) * wi[:, h:h + 1]
        key = jnp.where(kt * tk + lane_pos <= qpos, _order_key(acc), INT_MIN)
        key_scr[:, pl.ds(pl.multiple_of(kt * tk, tk), tk)] = key
        return carry

    lax.fori_loop(0, n_kt, score_tile, 0)

    def count(pred):
        def body(kt, acc):
            keys = key_scr[:, pl.ds(pl.multiple_of(kt * tk, tk), tk)]
            return acc + jnp.where(pred(keys, kt * tk + lane_pos), 1, 0)
        acc = lax.fori_loop(0, n_kt, body, jnp.zeros((tq, tk), I32))
        return jnp.sum(acc.astype(F32), axis=1, keepdims=True)

    thr, tie_end = _select_threshold(
        lambda t: count(lambda keys, pos: keys >= t),
        lambda t, j: count(lambda keys, pos: jnp.logical_and(keys == t, pos <= j)),
        qpos + 1, ktop, tq, idx_bits)

    q = q_ref[...]

    def attend(kt, carry):
        ms, ls, accs = carry
        start = pl.multiple_of(kt * tk, tk)
        keys = key_scr[:, pl.ds(start, tk)]
        sel = jnp.logical_or(keys > thr, jnp.logical_and(keys >= thr, kt * tk + lane_pos <= tie_end))
        kb = k_ref[pl.ds(start, tk), :]
        vb = v_ref[pl.ds(start, tk), :]
        ms2, ls2, accs2 = [], [], []
        for h in range(N_HEADS):
            sl = slice(h * HEAD_DIM, (h + 1) * HEAD_DIM)
            s = jnp.where(sel, _dot_nt(q[:, sl], kb[:, sl]), NEG)
            m_new = jnp.maximum(ms[h], jnp.max(s, axis=1, keepdims=True))
            alpha = jnp.exp(ms[h] - m_new)
            p = jnp.where(sel, jnp.exp(s - m_new), 0.0)
            ms2.append(m_new)
            ls2.append(alpha * ls[h] + jnp.sum(p, axis=1, keepdims=True))
            accs2.append(alpha * accs[h] + _dot(p.astype(BF16), vb[:, sl]))
        return tuple(ms2), tuple(ls2), tuple(accs2)

    init = (tuple(jnp.full((tq, 1), NEG, F32) for _ in range(N_HEADS)),
            tuple(jnp.zeros((tq, 1), F32) for _ in range(N_HEADS)),
            tuple(jnp.zeros((tq, HEAD_DIM), F32) for _ in range(N_HEADS)))
    _, ls, accs = lax.fori_loop(0, n_kt, attend, init)
    o_ref[...] = jnp.concatenate([accs[h] / ls[h] for h in range(N_HEADS)], axis=1)


def _dsa_prompt(qib, wi, qb, kib, kb, vb, n_seq, t, tq, tk):
    nq = t // tq
    ktop = min(TOPK_MAX, t // 4)
    qrow = lambda w: pl.BlockSpec((tq, w), lambda b, i: (b * nq + i, 0))
    seq = lambda w: pl.BlockSpec((t, w), lambda b, i: (b, 0))
    body = functools.partial(_dsa_prompt_body, ktop=ktop, tq=tq, tk=tk, idx_bits=max(1, (t - 1).bit_length()))
    return pl.pallas_call(
        body,
        grid=(n_seq, nq),
        in_specs=[qrow(D_ATT), qrow(N_IDX_HEADS), qrow(D_ATT), seq(IDX_DIM), seq(D_ATT), seq(D_ATT)],
        out_specs=qrow(D_ATT),
        out_shape=jax.ShapeDtypeStruct((n_seq * t, D_ATT), F32),
        scratch_shapes=[pltpu.VMEM((tq, t), I32)],
        compiler_params=_params("parallel", "arbitrary"),
        name="dsa_prompt",
    )(qib, wi, qb, kib, kb, vb)


def _idx_score(qi, wi_col, ki_bf):
    rel = jnp.maximum(_dot_nt(qi, ki_bf), 0.0) * wi_col
    return 0.0 + jnp.sum(rel, axis=0, keepdims=True)


def _sample_score_body(pt_ref, qi_ref, wi_ref, kin_ref, page_ref, sc_ref, scn_ref):
    qi = qi_ref[0]
    wi_col = wi_ref[0]
    sc_ref[0, 0] = _idx_score(qi, wi_col, page_ref[0].astype(BF16))

    @pl.when(pl.program_id(1) == 0)
    def _():
        kn = kin_ref[0].astype(BF16).astype(F32)
        rel = jnp.maximum(jnp.sum(qi.astype(F32) * kn, axis=1, keepdims=True), 0.0) * wi_col
        scn_ref[0] = jnp.broadcast_to(0.0 + jnp.sum(rel, axis=0, keepdims=True), (1, LANES))


def _sample_scores(page_table, qib, wi, ki_new, cache_kidx, n_seq, n_pages):
    grid_spec = pltpu.PrefetchScalarGridSpec(
        num_scalar_prefetch=1,
        grid=(n_seq, n_pages),
        in_specs=[pl.BlockSpec((1, N_IDX_HEADS, IDX_DIM), lambda b, j, pt: (b, 0, 0)),
                  pl.BlockSpec((1, N_IDX_HEADS, 1), lambda b, j, pt: (b, 0, 0)),
                  pl.BlockSpec((1, 1, IDX_DIM), lambda b, j, pt: (b, 0, 0)),
                  pl.BlockSpec((1, PAGE_SIZE, IDX_DIM), lambda b, j, pt: (pt[b * n_pages + j], 0, 0))],
        out_specs=[pl.BlockSpec((1, 1, 1, PAGE_SIZE), lambda b, j, pt: (b, j, 0, 0)),
                   pl.BlockSpec((1, 1, LANES), lambda b, j, pt: (b, 0, 0))])
    return pl.pallas_call(
        _sample_score_body,
        grid_spec=grid_spec,
        out_shape=[jax.ShapeDtypeStruct((n_seq, n_pages, 1, PAGE_SIZE), F32),
                   jax.ShapeDtypeStruct((n_seq, 1, LANES), F32)],
        compiler_params=_params("parallel", "arbitrary"),
        name="sample_scores",
    )(page_table.reshape(-1), qib.reshape(n_seq, N_IDX_HEADS, IDX_DIM), wi.reshape(n_seq, N_IDX_HEADS, 1),
      ki_new.reshape(n_seq, 1, IDX_DIM), cache_kidx)


def _sample_select_body(sc_ref, scn_ref, sel_ref, key_scr, *, ktop, past, idx_bits):
    rows = sc_ref.shape[0]
    n_t = past // LANES + 1
    lane_pos = lax.broadcasted_iota(I32, (1, LANES), 1)
    key_scr[:, :past] = _order_key(sc_ref[...])
    key_scr[:, past:] = jnp.where(lane_pos == 0, _order_key(scn_ref[...]), INT_MIN)

    def count(pred):
        def body(kt, acc):
            keys = key_scr[:, pl.ds(pl.multiple_of(kt * LANES, LANES), LANES)]
            return acc + jnp.where(pred(keys, kt * LANES + lane_pos), 1, 0)
        acc = lax.fori_loop(0, n_t, body, jnp.zeros((rows, LANES), I32))
        return jnp.sum(acc.astype(F32), axis=1, keepdims=True)

    thr, tie_end = _select_threshold(
        lambda t: count(lambda keys, pos: keys >= t),
        lambda t, j: count(lambda keys, pos: jnp.logical_and(keys == t, pos <= j)),
        jnp.full((rows, 1), past + 1, I32), ktop, rows, idx_bits)

    def write(kt, carry):
        start = pl.multiple_of(kt * LANES, LANES)
        keys = key_scr[:, pl.ds(start, LANES)]
        sel = jnp.logical_or(keys > thr, jnp.logical_and(keys >= thr, kt * LANES + lane_pos <= tie_end))
        sel_ref[:, pl.ds(start, LANES)] = jnp.where(sel, 1.0, 0.0)
        return carry

    lax.fori_loop(0, n_t, write, 0)


def _sample_select(scores, score_new, past):
    rows = scores.shape[0]
    width = past + LANES
    ktop = min(TOPK_MAX, (past + 1) // 4)
    body = functools.partial(_sample_select_body, ktop=ktop, past=past, idx_bits=width.bit_length())
    return pl.pallas_call(
        body,
        grid=(1,),
        in_specs=[_full((rows, past)), _full((rows, LANES))],
        out_specs=_full((rows, width)),
        out_shape=jax.ShapeDtypeStruct((rows, width), F32),
        scratch_shapes=[pltpu.VMEM((rows, width), I32)],
        compiler_params=_params("arbitrary"),
        name="sample_select",
    )(scores, score_new)


def _sample_attend_body(pt_ref, q_ref, sel_ref, seln_ref, kn_ref, vn_ref, kp_ref, vp_ref, o_ref,
                        m_scr, l_scr, acc_scr):
    j = pl.program_id(1)
    head_of_lane = lax.broadcasted_iota(I32, (N_HEADS, D_ATT), 1) // HEAD_DIM
    own = head_of_lane == lax.broadcasted_iota(I32, (N_HEADS, D_ATT), 0)
    q_bd = jnp.where(own, q_ref[0].astype(F32), 0.0).astype(BF16)

    @pl.when(j == 0)
    def _():
        kn = kn_ref[0].astype(BF16).astype(F32)
        logit = jnp.sum(q_bd.astype(F32) * kn, axis=1, keepdims=True)
        on = seln_ref[0, 0][:, 0:1] > 0.0
        m_scr[...] = jnp.where(on, logit, NEG)
        l_scr[...] = jnp.where(on, jnp.ones_like(logit), 0.0)
        acc_scr[...] = jnp.where(on, jnp.broadcast_to(vn_ref[0].astype(BF16).astype(F32), (N_HEADS, D_ATT)), 0.0)

    sel = sel_ref[0, 0] > 0.0
    s = jnp.where(sel, _dot_nt(q_bd, kp_ref[0].astype(BF16)), NEG)
    m_old = m_scr[...]
    m_new = jnp.maximum(m_old, jnp.max(s, axis=1, keepdims=True))
    alpha = jnp.exp(m_old - m_new)
    p = jnp.where(sel, jnp.exp(s - m_new), 0.0)
    m_scr[...] = m_new
    l_scr[...] = alpha * l_scr[...] + jnp.sum(p, axis=1, keepdims=True)
    acc_scr[...] = alpha * acc_scr[...] + _dot(p.astype(BF16), vp_ref[0].astype(BF16))

    @pl.when(j == pl.num_programs(1) - 1)
    def _():
        o_ref[0] = jnp.sum(jnp.where(own, acc_scr[...] / l_scr[...], 0.0), axis=0, keepdims=True)


def _sample_attend(page_table, qb, sel, k_new, v_new, cache_k, cache_v, n_seq, n_pages):
    n_pool = cache_k.shape[0]
    sel4 = sel.reshape(n_seq, n_pages + 1, 1, LANES)
    row = lambda: pl.BlockSpec((1, 1, D_ATT), lambda b, j, pt: (b, 0, 0))
    page = lambda: pl.BlockSpec((1, PAGE_SIZE, D_ATT), lambda b, j, pt: (pt[b * n_pages + j], 0, 0))
    grid_spec = pltpu.PrefetchScalarGridSpec(
        num_scalar_prefetch=1,
        grid=(n_seq, n_pages),
        in_specs=[row(),
                  pl.BlockSpec((1, 1, 1, LANES), lambda b, j, pt: (b, j, 0, 0)),
                  pl.BlockSpec((1, 1, 1, LANES), lambda b, j, pt: (b, n_pages, 0, 0)),
                  row(), row(), page(), page()],
        out_specs=row(),
        scratch_shapes=[pltpu.VMEM((N_HEADS, 1), F32), pltpu.VMEM((N_HEADS, 1), F32),
                        pltpu.VMEM((N_HEADS, D_ATT), F32)])
    out = pl.pallas_call(
        _sample_attend_body,
        grid_spec=grid_spec,
        out_shape=jax.ShapeDtypeStruct((n_seq, 1, D_ATT), F32),
        compiler_params=_params("parallel", "arbitrary"),
        name="sample_attend",
    )(page_table.reshape(-1), qb.reshape(n_seq, 1, D_ATT), sel4, sel4, k_new.reshape(n_seq, 1, D_ATT),
      v_new.reshape(n_seq, 1, D_ATT), cache_k.reshape(n_pool, PAGE_SIZE, D_ATT),
      cache_v.reshape(n_pool, PAGE_SIZE, D_ATT))
    return out.reshape(n_seq, D_ATT)


def _layer_norm(x, g, b):
    mu = jnp.mean(x, axis=-1, keepdims=True)
    d = x - mu
    var = jnp.mean(d * d, axis=-1, keepdims=True)
    return d * lax.rsqrt(var + LN_EPS) * g + b


def _post_body(x_ref, orw_ref, oatt_ref, wo_ref, g_ref, b_ref, wr_ref, h_o, gate_o):
    mix = _dot(orw_ref[...].astype(BF16), wo_ref[0:D_RWKV, :]) + _dot(oatt_ref[...].astype(BF16), wo_ref[D_RWKV:, :])
    h = _layer_norm(ALPHA * x_ref[...] + mix, g_ref[...], b_ref[...])
    h_o[...] = h
    logit = _dot_hi(h, wr_ref[...])
    rows = logit.shape[0]
    lane = lax.broadcasted_iota(I32, (rows, LANES), 1).astype(F32)
    big = float(LANES)

    def first_max(mask):
        val = jnp.max(jnp.where(mask, logit, -jnp.inf), axis=1, keepdims=True)
        idx = jnp.min(jnp.where(jnp.logical_and(mask, logit == val), lane, big), axis=1, keepdims=True)
        return val, idx

    is_grp = jnp.logical_and(lane >= N_EXPERTS, lane < N_EXPERTS + N_GROUPS)
    g_max, g_lane = first_max(is_grp)
    p_grp = 1.0 / jnp.sum(jnp.where(is_grp, jnp.exp(logit - g_max), 0.0), axis=1, keepdims=True)
    in_grp = jnp.floor(lane * (1.0 / EXPERTS_PER_GROUP)) == (g_lane - N_EXPERTS)
    v1, i1 = first_max(in_grp)
    v2, i2 = first_max(jnp.logical_and(in_grp, lane != i1))
    e2 = jnp.exp(v2 - v1)
    g1 = p_grp / (1.0 + e2)
    g2 = p_grp * e2 / (1.0 + e2)
    gate_o[...] = jnp.where(lane == i1, g1, 0.0) + jnp.where(lane == i2, g2, 0.0)


def _post(x, o_rw, o_att, pw, tm):
    n = x.shape[0]
    row = lambda w: pl.BlockSpec((tm, w), lambda i: (i, 0))
    return pl.pallas_call(
        _post_body,
        grid=(n // tm,),
        in_specs=[row(D_MODEL), row(D_RWKV), row(D_ATT), _full((D_MODEL, D_MODEL)), _full((1, D_MODEL)),
                  _full((1, D_MODEL)), _full((D_MODEL, LANES))],
        out_specs=[row(D_MODEL), row(LANES)],
        out_shape=[jax.ShapeDtypeStruct((n, D_MODEL), F32), jax.ShapeDtypeStruct((n, LANES), F32)],
        compiler_params=_params("parallel"),
        name="post",
    )(x, o_rw, o_att, pw["wout"], pw["ln1g"], pw["ln1b"], pw["wrouter"])


def _moe_body(h_ref, gate_ref, w1_ref, w3_ref, w2_ref, g_ref, b_ref, y_ref, hb_scr, acc_scr):
    e = pl.program_id(1)

    @pl.when(e == 0)
    def _():
        hb_scr[...] = h_ref[...].astype(BF16)
        acc_scr[...] = jnp.zeros_like(acc_scr)

    hb = hb_scr[...]
    gates = gate_ref[0]
    for j in range(EXPERT_CHUNK):
        h1 = _dot(hb, w1_ref[j])
        h3 = _dot(hb, w3_ref[j])
        hidden = jax.nn.silu(h1) * h3 * gates[:, j:j + 1]
        acc_scr[...] += _dot(hidden.astype(BF16), w2_ref[j])

    @pl.when(e == pl.num_programs(1) - 1)
    def _():
        y_ref[...] = _layer_norm(ALPHA * h_ref[...] + acc_scr[...], g_ref[...], b_ref[...])


def _moe(h, gate, pw, tm):
    n = h.shape[0]
    n_ec = N_EXPERTS // EXPERT_CHUNK
    gate_c = gate[:, :N_EXPERTS].reshape(n, n_ec, EXPERT_CHUNK).transpose(1, 0, 2)
    return pl.pallas_call(
        _moe_body,
        grid=(n // tm, n_ec),
        in_specs=[pl.BlockSpec((tm, D_MODEL), lambda i, e: (i, 0)),
                  pl.BlockSpec((1, tm, EXPERT_CHUNK), lambda i, e: (e, i, 0)),
                  pl.BlockSpec((EXPERT_CHUNK, D_MODEL, D_EXPERT), lambda i, e: (e, 0, 0)),
                  pl.BlockSpec((EXPERT_CHUNK, D_MODEL, D_EXPERT), lambda i, e: (e, 0, 0)),
                  pl.BlockSpec((EXPERT_CHUNK, D_EXPERT, D_MODEL), lambda i, e: (e, 0, 0)),
                  _full((1, D_MODEL)), _full((1, D_MODEL))],
        out_specs=pl.BlockSpec((tm, D_MODEL), lambda i, e: (i, 0)),
        out_shape=jax.ShapeDtypeStruct((n, D_MODEL), F32),
        scratch_shapes=[pltpu.VMEM((tm, D_MODEL), BF16), pltpu.VMEM((tm, D_MODEL), F32)],
        compiler_params=_params("parallel", "arbitrary"),
        name="moe",
    )(h, gate_c, pw["w1"], pw["w3"], pw["w2"], pw["ln2g"], pw["ln2b"])


def _prepare_weights(w_in, rw_mu, rw_w0, rw_w_up, rw_a0, rw_a_up, rw_g_up, rw_k_k, rw_k_a, rw_r_k,
                     rw_ln_g, rw_ln_b, idx_ln_g, idx_ln_b, w_out, ln1_g, ln1_b,
                     moe_w_grp, moe_w_exp, moe_w1, moe_w3, moe_w2, ln2_g, ln2_b):
    row = lambda z: z.reshape(1, -1).astype(F32)
    c5 = RW_COLS + 4 * D_ATT
    pad_to = lambda z, w: jnp.pad(z, ((0, 0), (0, w - z.shape[1])))
    wtail = jnp.concatenate([pad_to(w_in[:, c5:c5 + IDX_DIM], LANES),
                             pad_to(w_in[:, c5 + IDX_DIM:], LANES)], axis=1)
    zeros_w = jnp.zeros((LORA_W, D_RWKV), F32)
    return {
        "wrw": w_in[:, :RW_COLS].astype(BF16),
        "watt": w_in[:, RW_COLS:c5].astype(BF16),
        "wtail": wtail.astype(BF16),
        "mu": row(rw_mu), "w0": row(rw_w0), "a0": row(rw_a0), "kk": row(rw_k_k), "ka": row(rw_k_a),
        "wup": jnp.concatenate([rw_w_up, zeros_w], 0).astype(BF16),
        "aup": jnp.concatenate([zeros_w, rw_a_up], 0).astype(BF16),
        "gup": rw_g_up.astype(BF16),
        "lng": pad_to(row(idx_ln_g), LANES), "lnb": pad_to(row(idx_ln_b), LANES),
        "rk": row(rw_r_k), "rwlng": row(rw_ln_g), "rwlnb": row(rw_ln_b),
        "wout": w_out.astype(BF16), "ln1g": row(ln1_g), "ln1b": row(ln1_b),
        "wrouter": pad_to(jnp.concatenate([moe_w_exp, moe_w_grp], 1).astype(F32), LANES),
        "w1": moe_w1.astype(BF16), "w3": moe_w3.astype(BF16), "w2": moe_w2.astype(BF16),
        "ln2g": row(ln2_g), "ln2b": row(ln2_b),
    }


def _row_tile(n, want):
    return min(n, want)


def _layer(x, x_shift, pos_tables, pw, attend, scan, *, tm_proj, tm_post, tm_moe):
    n = x.shape[0]
    outs = _proj(x, x_shift, pos_tables, pw, _row_tile(n, tm_proj))
    rw = outs[:7]
    qb, kf, vf, kb, vb, qib, kif, kib, wi = outs[7:]
    o_rw, s_fin = scan(rw)
    o_att = attend(qb, kf, vf, kb, vb, qib, kif, kib, wi)
    h, gate = _post(x, o_rw, o_att, pw, _row_tile(n, tm_post))
    y = _moe(h, gate, pw, _row_tile(n, tm_moe))
    return y, kf, vf, kif, s_fin


def kernel(x_prompt, x_sample, cache_k, cache_v, cache_kidx, state_wkv, state_shift, page_table, w_in, rw_mu, rw_w0, rw_w_up, rw_a0, rw_a_up, rw_g_up, rw_k_k, rw_k_a, rw_r_k, rw_ln_g, rw_ln_b, idx_ln_g, idx_ln_b, w_out, ln1_g, ln1_b, moe_w_grp, moe_w_exp, moe_w1, moe_w3, moe_w2, ln2_g, ln2_b):
    pw = _prepare_weights(w_in, rw_mu, rw_w0, rw_w_up, rw_a0, rw_a_up, rw_g_up, rw_k_k, rw_k_a, rw_r_k,
                          rw_ln_g, rw_ln_b, idx_ln_g, idx_ln_b, w_out, ln1_g, ln1_b,
                          moe_w_grp, moe_w_exp, moe_w1, moe_w3, moe_w2, ln2_g, ln2_b)
    bp, tp, _ = x_prompt.shape
    bs, ts, _ = x_sample.shape
    assert ts == 1
    n_pages = page_table.shape[1]
    past = n_pages * PAGE_SIZE

    xp = x_prompt.reshape(bp * tp, D_MODEL)
    xp_shift = jnp.concatenate([jnp.zeros((bp, 1, D_MODEL), x_prompt.dtype), x_prompt[:, :-1]], axis=1)
    tq = min(tp, 256)

    def attend_p(qb, kf, vf, kb, vb, qib, kif, kib, wi):
        return _dsa_prompt(qib, wi, qb, kib, kb, vb, bp, tp, tq, tq)

    def scan_p(rw):
        s0 = jnp.zeros((bp, N_HEADS, HEAD_DIM, HEAD_DIM), F32)
        return _scan_chunked(rw, s0, pw["rk"], pw["rwlng"], pw["rwlnb"], bp, tp)

    y_p, k_p, v_p, ki_p, wkv_p = _layer(xp, xp_shift.reshape(bp * tp, D_MODEL), _rope_tables(jnp.arange(tp)),
                                        pw, attend_p, scan_p, tm_proj=256, tm_post=512, tm_moe=1024)

    xs = x_sample.reshape(bs, D_MODEL)

    def attend_s(qb, kf, vf, kb, vb, qib, kif, kib, wi):
        scores, score_new = _sample_scores(page_table, qib, wi, kif.astype(cache_kidx.dtype), cache_kidx, bs, n_pages)
        sel = _sample_select(scores.reshape(bs, past), score_new.reshape(bs, LANES), past)
        return _sample_attend(page_table, qb, sel, kf.astype(cache_k.dtype), vf.astype(cache_v.dtype),
                              cache_k, cache_v, bs, n_pages)

    def scan_s(rw):
        return _scan_step(rw, state_wkv.astype(F32), pw["rk"], pw["rwlng"], pw["rwlnb"], bs)

    y_s, k_s, v_s, ki_s, wkv_s = _layer(xs, state_shift.astype(x_sample.dtype),
                                        _rope_tables(jnp.full((bs,), past, I32)),
                                        pw, attend_s, scan_s, tm_proj=256, tm_post=512, tm_moe=1024)

    heads = lambda z, b, t: z.reshape(b, t, N_HEADS, HEAD_DIM)
    return (y_p.reshape(bp, tp, D_MODEL), y_s.reshape(bs, ts, D_MODEL),
            heads(k_p, bp, tp).astype(cache_k.dtype), heads(v_p, bp, tp).astype(cache_v.dtype),
            ki_p.reshape(bp, tp, IDX_DIM).astype(cache_kidx.dtype),
            wkv_p.astype(state_wkv.dtype), x_prompt[:, -1].astype(state_shift.dtype),
            heads(k_s, bs, ts).astype(cache_k.dtype), heads(v_s, bs, ts).astype(cache_v.dtype),
            ki_s.reshape(bs, ts, IDX_DIM).astype(cache_kidx.dtype),
            wkv_s.astype(state_wkv.dtype), x_sample[:, -1].astype(state_shift.dtype))
```

```python
import functools
import math

import jax
import jax.numpy as jnp
from jax import lax
from jax.experimental import pallas as pl
from jax.experimental.pallas import tpu as pltpu

F32 = jnp.float32
BF16 = jnp.bfloat16
I32 = jnp.int32

D_MODEL = 1024
HEAD_DIM = 64
D_RWKV = 512
D_ATT = 512
N_HEADS = 8
LORA_W = 64
LORA_A = 64
LORA_G = 128
N_IDX_HEADS = 8
IDX_DIM = 64
TOPK_MAX = 256
ROPE_THETA = 500000.0
PAGE_SIZE = 128
N_GROUPS = 4
EXPERTS_PER_GROUP = 8
N_EXPERTS = 32
D_EXPERT = 256
LN_EPS = 1e-5
GN_EPS = HEAD_DIM * 1e-5
DECAY_SCALE = math.exp(-0.5)
ALPHA = 2.0 ** 0.25
NEG = -1e30
RW_COLS = 3 * D_RWKV + LORA_W + LORA_A + LORA_G
WI_SCALE = N_IDX_HEADS ** -0.5 * IDX_DIM ** -0.5
QK_SCALE = HEAD_DIM ** -0.5 * math.log2(math.e)
INT_MIN = -(2 ** 31)
INT_MAX = 2 ** 31 - 1

LANES = 128
SUBLANES = 8
VMEM_LIMIT = 56 * 1024 * 1024
SCAN_CHUNK = 64
SOLVE_BLOCK = 16
EXPERT_CHUNK = 4
DSA_TILE = 256
ACC_ROWS = HEAD_DIM + 16
SCORE_PAGES = 8
ATTEND_PAGES = 4
PAGE_ROWS = PAGE_SIZE * N_HEADS


def _dot(a, b):
    return jnp.dot(a, b, preferred_element_type=F32)


def _dot_nt(a, b):
    return lax.dot_general(a, b, (((1,), (1,)), ((), ())), preferred_element_type=F32)


def _dot_hi(a, b):
    return jnp.dot(a, b, preferred_element_type=F32, precision=lax.Precision.HIGHEST)


def _split(x):
    hi = x.astype(BF16)
    return hi, (x - hi.astype(F32)).astype(BF16)


def _dot3(a, b, nt=False):
    f = _dot_nt if nt else _dot
    ah, al = _split(a)
    bh, bl = _split(b)
    return f(ah, bh) + (f(ah, bl) + f(al, bh))


def _params(*sem):
    return pltpu.CompilerParams(dimension_semantics=sem, vmem_limit_bytes=VMEM_LIMIT)


def _full(shape):
    nd = len(shape)
    return pl.BlockSpec(shape, lambda *_: (0,) * nd)


def _rope(x, cc, s1, s2):
    outs = []
    for c in range(x.shape[1] // LANES):
        xc = x[:, c * LANES:(c + 1) * LANES]
        outs.append(xc * cc + pltpu.roll(xc, LANES - 8, 1) * s1 + pltpu.roll(xc, 8, 1) * s2)
    return outs[0] if len(outs) == 1 else jnp.concatenate(outs, axis=1)


def _proj_body(x_ref, xs_ref, wrw_ref, watt_ref, wtail_ref, mu_ref, w0_ref, wup_ref, a0_ref, aup_ref,
               gup_ref, kk_ref, ka_ref, lng_ref, lnb_ref, cc_ref, s1_ref, s2_ref,
               r_o, lw_o, k_o, v_o, kkr_o, a_o, g_o, qb_o, kf_o, vf_o, kb_o, vt_o, qib_o, kif_o, kib_o, wi_o):
    xb = x_ref[...].astype(BF16)
    xsb = xs_ref[...].astype(BF16)

    def lerp(c0, c1):
        w = wrw_ref[:, c0:c1]
        cur = _dot(xb, w)
        prev = _dot(xsb, w)
        return cur + mu_ref[:, c0:c1] * (prev - cur)

    r_o[...] = lerp(0, D_RWKV)
    k = lerp(D_RWKV, 2 * D_RWKV)
    v_o[...] = lerp(2 * D_RWKV, 3 * D_RWKV)
    lora = lerp(3 * D_RWKV, 3 * D_RWKV + LORA_W + LORA_A)
    dg = lerp(3 * D_RWKV + LORA_W + LORA_A, RW_COLS)
    lw_o[...] = -DECAY_SCALE * jax.nn.sigmoid(w0_ref[...] + _dot(jnp.tanh(lora).astype(BF16), wup_ref[...]))
    a = jax.nn.sigmoid(a0_ref[...] + _dot(lora.astype(BF16), aup_ref[...]))
    a_o[...] = a
    g_o[...] = _dot(jax.nn.sigmoid(dg).astype(BF16), gup_ref[...])
    kkr_o[...] = k * kk_ref[...]
    k_o[...] = k * (1.0 + (a - 1.0) * ka_ref[...])

    cc, s1, s2 = cc_ref[...], s1_ref[...], s2_ref[...]
    q = _rope(_dot(xb, watt_ref[:, 0:D_ATT]), cc, s1, s2)
    qb_o[...] = (q * QK_SCALE).astype(BF16)
    ka = _rope(_dot(xb, watt_ref[:, D_ATT:2 * D_ATT]), cc, s1, s2)
    kf_o[...] = ka
    kb_o[...] = ka.astype(BF16)
    va = _dot(xb, watt_ref[:, 2 * D_ATT:3 * D_ATT])
    vf_o[...] = va
    vt_o[...] = va.T.astype(BF16)
    qi = _rope(_dot(xb, watt_ref[:, 3 * D_ATT:4 * D_ATT]), cc, s1, s2)
    qib_o[...] = qi.astype(BF16)

    tail = _dot(xb, wtail_ref[...])
    t0 = tail[:, :LANES]
    in_ki = lax.broadcasted_iota(I32, (1, LANES), 1) < IDX_DIM
    mean = jnp.sum(jnp.where(in_ki, t0, 0.0), axis=1, keepdims=True) * (1.0 / IDX_DIM)
    d = jnp.where(in_ki, t0 - mean, 0.0)
    var = jnp.sum(d * d, axis=1, keepdims=True) * (1.0 / IDX_DIM)
    ki = _rope(d * lax.rsqrt(var + LN_EPS) * lng_ref[...] + lnb_ref[...], cc, s1, s2)[:, :IDX_DIM]
    kif_o[...] = ki
    kib_o[...] = ki.astype(BF16)
    wi_o[...] = tail[:, LANES:LANES + N_IDX_HEADS] * WI_SCALE


def _rope_tables(pos):
    half = HEAD_DIM // 8
    inv = ROPE_THETA ** (-jnp.arange(half, dtype=F32) / half)
    ang = pos.astype(F32)[:, None] * inv[None, :]
    cos, sin = jnp.cos(ang), jnp.sin(ang)
    n = pos.shape[0]
    one = jnp.ones((n, HEAD_DIM - 2 * half), F32)
    zero = jnp.zeros((n, HEAD_DIM - 2 * half), F32)
    zh = jnp.zeros((n, half), F32)
    cc = jnp.concatenate([cos, cos, one], 1)
    s1 = jnp.concatenate([-sin, zh, zero], 1)
    s2 = jnp.concatenate([zh, sin, zero], 1)
    return tuple(jnp.tile(t, (1, LANES // HEAD_DIM)) for t in (cc, s1, s2))


def _proj(x, xs, tables, pw, tm):
    n = x.shape[0]
    nt = tables[0].shape[0] // tm
    row = lambda w: pl.BlockSpec((tm, w), lambda i: (i, 0))
    tab = pl.BlockSpec((tm, LANES), lambda i: (i % nt, 0))
    weights = (pw["wrw"], pw["watt"], pw["wtail"], pw["mu"], pw["w0"], pw["wup"], pw["a0"], pw["aup"],
               pw["gup"], pw["kk"], pw["ka"], pw["lng"], pw["lnb"])
    f = lambda w, dt=F32: jax.ShapeDtypeStruct((n, w), dt)
    out_shape = ([f(D_RWKV)] * 7 + [f(D_ATT, BF16), f(D_ATT), f(D_ATT), f(D_ATT, BF16),
                                    jax.ShapeDtypeStruct((D_ATT, n), BF16),
                                    f(D_ATT, BF16), f(IDX_DIM), f(IDX_DIM, BF16), f(N_IDX_HEADS)])
    out_specs = ([row(D_RWKV)] * 7 + [row(D_ATT)] * 4 + [pl.BlockSpec((D_ATT, tm), lambda i: (0, i))]
                 + [row(D_ATT), row(IDX_DIM), row(IDX_DIM), row(N_IDX_HEADS)])
    return pl.pallas_call(
        _proj_body,
        grid=(n // tm,),
        in_specs=[row(D_MODEL), row(D_MODEL)] + [_full(w.shape) for w in weights] + [tab] * 3,
        out_specs=out_specs,
        out_shape=out_shape,
        compiler_params=_params("parallel"),
        name="proj",
    )(x, xs, *weights, *tables)


def _head_epilogue(o, r, k, v, g, rk, lng, lnb):
    mu = jnp.mean(o, axis=-1, keepdims=True)
    d = o - mu
    var = jnp.mean(d * d, axis=-1, keepdims=True)
    on = d * lax.rsqrt(var + GN_EPS) * lng + lnb
    bonus = jnp.sum(r * k * rk, axis=-1, keepdims=True) * v
    return (on + bonus) * g


def _scan_chunk_body(r_ref, lw_ref, k_ref, v_ref, kk_ref, a_ref, g_ref, rk_ref, lng_ref, lnb_ref, s0_ref,
                     o_ref, sf_ref, s_scr):
    c = pl.program_id(1)

    @pl.when(c == 0)
    def _():
        s_scr[...] = s0_ref[0]

    C = r_ref.shape[0]
    heads = range(N_HEADS)
    row = lax.broadcasted_iota(I32, (C, 2 * C), 0)
    col = lax.broadcasted_iota(I32, (C, 2 * C), 1) % C
    strict2 = col < row
    incl2 = col <= row
    same_blk = (col // SOLVE_BLOCK) == (row // SOLVE_BLOCK)
    left = lax.broadcasted_iota(I32, (C, 2 * C), 1) < C
    tri = incl2[:, :C].astype(BF16)

    lw_all = lw_ref[...]
    l_hi = lw_all.astype(BF16)
    l_r1 = lw_all - l_hi.astype(F32)
    l_mid = l_r1.astype(BF16)
    l_lo = (l_r1 - l_mid.astype(F32)).astype(BF16)
    cum_all = _dot(tri, l_hi) + (_dot(tri, l_mid) + _dot(tri, l_lo))

    sls = [slice(h * HEAD_DIM, (h + 1) * HEAD_DIM) for h in heads]
    r = [r_ref[:, sl] for sl in sls]
    k = [k_ref[:, sl] for sl in sls]
    v = [v_ref[:, sl] for sl in sls]
    lhs1, rhs1, bk_end, g_end = [], [], [], []
    for h in heads:
        sl = sls[h]
        lw, kkr, a = lw_all[:, sl], kk_ref[:, sl], a_ref[:, sl]
        cum = cum_all[:, sl]
        cum_end = cum[C - 1:C, :]
        kk = kkr * lax.rsqrt(jnp.maximum(jnp.sum(kkr * kkr, axis=-1, keepdims=True), 1e-24))
        beta = kk * a
        e_neg = jnp.exp(-cum)
        e_end = jnp.exp(cum_end - cum)
        lhs1.append(jnp.concatenate([-kk * jnp.exp(cum - lw), r[h] * jnp.exp(cum)], axis=0))
        rhs1.append(jnp.concatenate([beta * e_neg, k[h] * e_neg, s_scr[h]], axis=0))
        bk_end.append(jnp.concatenate([beta * e_end, k[h] * e_end], axis=0))
        g_end.append(jnp.exp(cum_end))

    r1 = [_dot3(lhs1[h], rhs1[h], nt=True) for h in heads]
    vv = [jnp.concatenate([v[h], v[h]], axis=0) for h in heads]
    y = [r1[h][:C, 2 * C:] + _dot3(jnp.where(jnp.logical_and(strict2, ~left), r1[h][:C, :2 * C], 0.0), vv[h])
         for h in heads]
    n_pair = [r1[h][:C, :2 * C] for h in heads]
    p = [jnp.where(jnp.logical_and(strict2, same_blk), n_pair[h], 0.0)[:, :C] for h in heads]
    x = [jnp.where(left, jnp.where(jnp.logical_and(strict2, ~same_blk), n_pair[h], 0.0),
                   jnp.concatenate([y[h], y[h]], axis=1)) for h in heads]
    n_steps = SOLVE_BLOCK.bit_length() - 1
    for i in range(n_steps):
        if i + 1 < n_steps:
            rr = [_dot3(p[h], jnp.concatenate([x[h], p[h]], axis=1)) for h in heads]
            x = [x[h] + rr[h][:, :2 * C] for h in heads]
            p = [rr[h][:, 2 * C:] for h in heads]
        else:
            x = [x[h] + _dot3(p[h], x[h]) for h in heads]
    rr = [_dot3(x[h][:, :C], x[h]) for h in heads]
    x = [jnp.where(left, rr[h], x[h] + rr[h]) for h in heads]
    rr = [_dot3(x[h][:, :C], x[h]) for h in heads]
    u = [(x[h] + rr[h])[:, C:] for h in heads]

    uv = [jnp.concatenate([u[h], v[h]], axis=0) for h in heads]
    o = [r1[h][C:, 2 * C:] + _dot3(jnp.where(incl2, r1[h][C:, :2 * C], 0.0), uv[h]) for h in heads]
    for h in heads:
        s_new = rhs1[h][2 * C:, :] * g_end[h] + _dot3(uv[h].T, bk_end[h])
        s_scr[h] = s_new
        sl = sls[h]
        o_ref[:, sl] = _head_epilogue(o[h], r[h], k[h], v[h], g_ref[:, sl], rk_ref[:, sl], lng_ref[:, sl],
                                      lnb_ref[:, sl])

    @pl.when(c == pl.num_programs(1) - 1)
    def _():
        sf_ref[0] = s_scr[...]


def _scan_chunked(rw, s0, rk, lng, lnb, n_seq, t):
    nc = t // SCAN_CHUNK
    row = pl.BlockSpec((SCAN_CHUNK, D_RWKV), lambda b, c: (b * nc + c, 0))
    st = pl.BlockSpec((1, N_HEADS, HEAD_DIM, HEAD_DIM), lambda b, c: (b, 0, 0, 0))
    par = _full((1, D_RWKV))
    return pl.pallas_call(
        _scan_chunk_body,
        grid=(n_seq, nc),
        in_specs=[row] * 7 + [par] * 3 + [st],
        out_specs=[row, st],
        out_shape=[jax.ShapeDtypeStruct((n_seq * t, D_RWKV), F32),
                   jax.ShapeDtypeStruct((n_seq, N_HEADS, HEAD_DIM, HEAD_DIM), F32)],
        scratch_shapes=[pltpu.VMEM((N_HEADS, HEAD_DIM, HEAD_DIM), F32)],
        compiler_params=_params("parallel", "arbitrary"),
        name="scan_chunked",
    )(*rw, rk, lng, lnb, s0)


def _scan_step_body(r_ref, lw_ref, k_ref, v_ref, kk_ref, a_ref, g_ref, rk_ref, lng_ref, lnb_ref, s0_ref,
                    o_ref, sf_ref):
    eye = (lax.broadcasted_iota(I32, (HEAD_DIM, HEAD_DIM), 0)
           == lax.broadcasted_iota(I32, (HEAD_DIM, HEAD_DIM), 1)).astype(F32)
    outs = []
    for h in range(N_HEADS):
        sl = slice(h * HEAD_DIM, (h + 1) * HEAD_DIM)
        r, lw, k, v = r_ref[0, :, sl], lw_ref[0, :, sl], k_ref[0, :, sl], v_ref[0, :, sl]
        kkr, a = kk_ref[0, :, sl], a_ref[0, :, sl]
        kk = kkr * lax.rsqrt(jnp.maximum(jnp.sum(kkr * kkr, axis=-1, keepdims=True), 1e-24))
        s = s0_ref[0, h]
        sa = jnp.sum(s * kk, axis=1, keepdims=True)
        v_col = jnp.sum(eye * v, axis=1, keepdims=True)
        s = s * jnp.exp(lw) - sa * (kk * a) + v_col * k
        sf_ref[0, h] = s
        o_col = jnp.sum(s * r, axis=1, keepdims=True)
        o = jnp.sum(eye * o_col, axis=0, keepdims=True)
        outs.append(_head_epilogue(o, r, k, v, g_ref[0, :, sl], rk_ref[:, sl], lng_ref[:, sl], lnb_ref[:, sl]))
    o_ref[0] = jnp.concatenate(outs, axis=1)


def _scan_step(rw, s0, rk, lng, lnb, n_seq):
    row = pl.BlockSpec((1, 1, D_RWKV), lambda b: (b, 0, 0))
    st = pl.BlockSpec((1, N_HEADS, HEAD_DIM, HEAD_DIM), lambda b: (b, 0, 0, 0))
    par = _full((1, D_RWKV))
    rw3 = [z.reshape(n_seq, 1, D_RWKV) for z in rw]
    o, sf = pl.pallas_call(
        _scan_step_body,
        grid=(n_seq,),
        in_specs=[row] * 7 + [par] * 3 + [st],
        out_specs=[row, st],
        out_shape=[jax.ShapeDtypeStruct((n_seq, 1, D_RWKV), F32),
                   jax.ShapeDtypeStruct((n_seq, N_HEADS, HEAD_DIM, HEAD_DIM), F32)],
        compiler_params=_params("parallel"),
        name="scan_step",
    )(*rw3, rk, lng, lnb, s0)
    return o.reshape(n_seq, D_RWKV), sf


def _order_key(score):
    bits = pltpu.bitcast(score, I32)
    return bits ^ ((bits >> 31) & INT_MAX)


def _select_threshold(count_ge, count_tie_le, n_valid, ktop, idx_bits):
    shape = n_valid.shape
    lo = jnp.full(shape, INT_MIN + 1, I32)
    hi = jnp.where(n_valid <= ktop, lo + 1, INT_MAX)

    def cond(st):
        it, lo, hi = st
        return jnp.logical_and(it < 33, jnp.max(jnp.where(lo + 1 != hi, 1.0, 0.0)) > 0.0)

    def body(st):
        it, lo, hi = st
        mid = (lo & hi) + ((lo ^ hi) >> 1)
        cnt = count_ge(mid)
        ge = cnt >= ktop
        lo2 = jnp.where(ge, mid, lo)
        hi2 = jnp.where(cnt == ktop, mid + 1, jnp.where(ge, hi, mid))
        return it + 1, lo2, hi2

    _, thr, _ = lax.while_loop(cond, body, (jnp.int32(0), lo, hi))
    excess = jnp.logical_and(count_ge(thr) > ktop, n_valid > ktop)
    has_tie = jnp.max(jnp.where(excess, 1.0, 0.0)) > 0.0
    all_idx = jnp.full(shape, (1 << idx_bits) - 1, I32)

    def tie_search():
        need = ktop - count_ge(thr + 1)

        def tbody(_, st):
            lo_j, hi_j = st
            mid = (lo_j + hi_j) >> 1
            ok = count_tie_le(thr, mid) >= need
            return jnp.where(ok, lo_j, mid + 1), jnp.where(ok, mid, hi_j)

        lo_j, _ = lax.fori_loop(0, idx_bits, tbody, (jnp.zeros(shape, I32), all_idx))
        return jnp.where(excess, lo_j, all_idx)

    tie_end = lax.cond(has_tie, tie_search, lambda: all_idx)
    return thr, tie_end, has_tie


def _dsa_prompt_body(qi_ref, wit_ref, q_ref, ki_ref, k_ref, vt_ref, o_ref, key_scr, m_scr, bias_scr, s_scr, acc_scr,
                     *, ktop, tq, tk, idx_bits):
    i = pl.program_id(1)
    n_kt = (i + 1) * (tq // tk)
    qpos = i * tq + lax.broadcasted_iota(I32, (1, tq), 1)
    key_off = lax.broadcasted_iota(I32, (tk, 1), 0)
    tile = lambda kt: pl.ds(pl.multiple_of(kt * tk, tk), tk)
    qi = qi_ref[...]
    qi_h = [qi[:, h * IDX_DIM:(h + 1) * IDX_DIM] for h in range(N_IDX_HEADS)]
    wit = wit_ref[...]

    def score_tile(kt, carry):
        kis = ki_ref[tile(kt), :]
        acc = jnp.zeros((tk, tq), F32)
        for h in range(N_IDX_HEADS):
            acc = acc + jnp.maximum(_dot_nt(kis, qi_h[h]), 0.0) * wit[h:h + 1, :]
        key_scr[tile(kt), :] = jnp.where(kt * tk + key_off <= qpos, _order_key(acc), INT_MIN)
        return carry

    lax.fori_loop(0, n_kt, score_tile, 0)

    def count(pred):
        def body(kt, acc):
            hit = jnp.where(pred(key_scr[tile(kt), :], kt * tk + key_off), 1.0, 0.0)
            return acc + jnp.sum(hit.reshape(tk // SUBLANES, SUBLANES, tq), axis=0)
        acc = lax.fori_loop(0, n_kt, body, jnp.zeros((SUBLANES, tq), F32))
        return jnp.sum(acc, axis=0, keepdims=True)

    thr, tie_end, has_tie = _select_threshold(
        lambda t: count(lambda keys, pos: keys >= t),
        lambda t, j: count(lambda keys, pos: jnp.logical_and(keys == t, pos <= j)),
        qpos + 1, ktop, idx_bits)

    @pl.when(has_tie)
    def _():
        def demote(kt, carry):
            keys = key_scr[tile(kt), :]
            late = jnp.logical_and(keys == thr, kt * tk + key_off > tie_end)
            key_scr[tile(kt), :] = jnp.where(late, thr - 1, keys)
            return carry
        lax.fori_loop(0, n_kt, demote, 0)

    q = q_ref[...]
    lane = lax.broadcasted_iota(I32, (1, LANES), 1)
    q_pad = []
    for h in range(N_HEADS):
        own = (lane < HEAD_DIM) if h % 2 == 0 else (lane >= HEAD_DIM)
        q_pad.append(jnp.where(own, q[:, (h // 2) * LANES:(h // 2 + 1) * LANES].astype(F32), 0.0).astype(BF16))
    m_scr[...] = jnp.full(m_scr.shape, NEG, F32)
    acc_scr[...] = jnp.zeros(acc_scr.shape, F32)
    ones = jnp.ones((ACC_ROWS - HEAD_DIM, tk), BF16)

    def attend(kt, carry):
        bias_scr[...] = jnp.where(key_scr[tile(kt), :] >= thr, 0.0, NEG)
        tile_max = []
        for h in range(N_HEADS):
            s = _dot_nt(k_ref[tile(kt), (h // 2) * LANES:(h // 2 + 1) * LANES], q_pad[h]) + bias_scr[...]
            s_scr[h] = s
            tile_max.append(jnp.max(s, axis=0, keepdims=True))
        m_old = m_scr[...]
        m_new = jnp.maximum(m_old, jnp.concatenate(tile_max, axis=0))
        alpha = jnp.exp2(m_old - m_new)
        m_scr[...] = m_new
        for h in range(N_HEADS):
            p = jnp.exp2(s_scr[h] - m_new[h:h + 1, :]).astype(BF16)
            v_ext = jnp.concatenate([vt_ref[h * HEAD_DIM:(h + 1) * HEAD_DIM, tile(kt)], ones], axis=0)
            rows = slice(h * ACC_ROWS, (h + 1) * ACC_ROWS)
            acc_scr[rows, :] = alpha[h:h + 1, :] * acc_scr[rows, :] + _dot(v_ext, p)
        return carry

    lax.fori_loop(0, n_kt, attend, 0)
    out_t = jnp.concatenate(
        [acc_scr[h * ACC_ROWS:h * ACC_ROWS + HEAD_DIM, :] / acc_scr[h * ACC_ROWS + HEAD_DIM:h * ACC_ROWS + HEAD_DIM + 1, :]
         for h in range(N_HEADS)], axis=0)
    o_ref[...] = out_t.T


def _dsa_prompt(qib, wit, qb, kib, kb, vt, n_seq, t):
    tq = min(t, DSA_TILE)
    nq = t // tq
    ktop = min(TOPK_MAX, t // 4)
    qrow = lambda w: pl.BlockSpec((tq, w), lambda b, i: (b * nq + i, 0))
    seq = lambda w: pl.BlockSpec((t, w), lambda b, i: (b, 0))
    body = functools.partial(_dsa_prompt_body, ktop=ktop, tq=tq, tk=tq, idx_bits=max(1, (t - 1).bit_length()))
    return pl.pallas_call(
        body,
        grid=(n_seq, nq),
        in_specs=[qrow(D_ATT), pl.BlockSpec((N_IDX_HEADS, tq), lambda b, i: (0, b * nq + i)), qrow(D_ATT),
                  seq(IDX_DIM), seq(D_ATT), pl.BlockSpec((D_ATT, t), lambda b, i: (0, b))],
        out_specs=qrow(D_ATT),
        out_shape=jax.ShapeDtypeStruct((n_seq * t, D_ATT), F32),
        scratch_shapes=[pltpu.VMEM((t, tq), I32), pltpu.VMEM((N_HEADS, tq), F32), pltpu.VMEM((tq, tq), F32),
                        pltpu.VMEM((N_HEADS, tq, tq), F32), pltpu.VMEM((N_HEADS * ACC_ROWS, tq), F32)],
        compiler_params=_params("parallel", "arbitrary"),
        name="dsa_prompt",
    )(qib, wit, qb, kib, kb, vt)


def _sample_score_body(pt_ref, qi_ref, wi_ref, kin_ref, *rest):
    page_refs, (sc_ref, scn_ref) = rest[:SCORE_PAGES], rest[SCORE_PAGES:]
    qi = qi_ref[0]
    wi_col = wi_ref[0]
    pages = jnp.concatenate([p[0] for p in page_refs], axis=0).astype(BF16)
    rel = jnp.maximum(_dot_nt(qi, pages), 0.0) * wi_col
    sc_ref[0] = 0.0 + jnp.sum(rel, axis=0, keepdims=True)

    @pl.when(pl.program_id(1) == 0)
    def _():
        kn = kin_ref[0].astype(BF16).astype(F32)
        rel_n = jnp.maximum(jnp.sum(qi.astype(F32) * kn, axis=1, keepdims=True), 0.0) * wi_col
        scn_ref[0] = jnp.broadcast_to(0.0 + jnp.sum(rel_n, axis=0, keepdims=True), (1, LANES))


def _sample_scores(page_table, qib, wi, ki_new, cache_kidx, n_seq, n_pages):
    g = SCORE_PAGES
    page = lambda s: pl.BlockSpec((1, PAGE_SIZE, IDX_DIM), lambda b, j, pt: (pt[b * n_pages + j * g + s], 0, 0))
    grid_spec = pltpu.PrefetchScalarGridSpec(
        num_scalar_prefetch=1,
        grid=(n_seq, n_pages // g),
        in_specs=[pl.BlockSpec((1, N_IDX_HEADS, IDX_DIM), lambda b, j, pt: (b, 0, 0)),
                  pl.BlockSpec((1, N_IDX_HEADS, 1), lambda b, j, pt: (b, 0, 0)),
                  pl.BlockSpec((1, 1, IDX_DIM), lambda b, j, pt: (b, 0, 0))] + [page(s) for s in range(g)],
        out_specs=[pl.BlockSpec((1, 1, g * PAGE_SIZE), lambda b, j, pt: (b, 0, j)),
                   pl.BlockSpec((1, 1, LANES), lambda b, j, pt: (b, 0, 0))])
    return pl.pallas_call(
        _sample_score_body,
        grid_spec=grid_spec,
        out_shape=[jax.ShapeDtypeStruct((n_seq, 1, n_pages * PAGE_SIZE), F32),
                   jax.ShapeDtypeStruct((n_seq, 1, LANES), F32)],
        compiler_params=_params("parallel", "arbitrary"),
        name="sample_scores",
    )(page_table.reshape(-1), qib.reshape(n_seq, N_IDX_HEADS, IDX_DIM), wi.reshape(n_seq, N_IDX_HEADS, 1),
      ki_new.reshape(n_seq, 1, IDX_DIM), *([cache_kidx] * g))


def _sample_select_body(sc_ref, scn_ref, spread_ref, sel_ref, key_scr, *, ktop, past, idx_bits):
    rows = sc_ref.shape[0]
    n_t = past // LANES + 1
    lane_pos = lax.broadcasted_iota(I32, (1, LANES), 1)
    tile = lambda kt: pl.ds(pl.multiple_of(kt * LANES, LANES), LANES)
    key_scr[:, :past] = _order_key(sc_ref[...])
    key_scr[:, past:] = jnp.where(lane_pos == 0, _order_key(scn_ref[...]), INT_MIN)

    def count(pred):
        def body(kt, acc):
            return acc + jnp.where(pred(key_scr[:, tile(kt)], kt * LANES + lane_pos), 1.0, 0.0)
        acc = lax.fori_loop(0, n_t, body, jnp.zeros((rows, LANES), F32))
        return jnp.sum(acc, axis=1, keepdims=True)

    thr, tie_end, _ = _select_threshold(
        lambda t: count(lambda keys, pos: keys >= t),
        lambda t, j: count(lambda keys, pos: jnp.logical_and(keys == t, pos <= j)),
        jnp.full((rows, 1), past + 1, I32), ktop, idx_bits)
    spread = spread_ref[...]

    def write(kt, carry):
        keys = key_scr[:, tile(kt)]
        sel = jnp.logical_or(keys > thr, jnp.logical_and(keys >= thr, kt * LANES + lane_pos <= tie_end))
        wide = pl.ds(pl.multiple_of(kt * PAGE_ROWS, PAGE_ROWS), PAGE_ROWS)
        sel_ref[:, wide] = _dot(jnp.where(sel, 1.0, 0.0).astype(BF16), spread)
        return carry

    lax.fori_loop(0, n_t, write, 0)


def _sample_select(scores, score_new, past):
    rows = scores.shape[0]
    width = past + LANES
    ktop = min(TOPK_MAX, (past + 1) // 4)
    spread = (jnp.arange(PAGE_ROWS)[None, :] // N_HEADS == jnp.arange(PAGE_SIZE)[:, None]).astype(BF16)
    body = functools.partial(_sample_select_body, ktop=ktop, past=past, idx_bits=width.bit_length())
    rb = min(rows, SUBLANES)
    blk = lambda w: pl.BlockSpec((rb, w), lambda i: (i, 0))
    return pl.pallas_call(
        body,
        grid=(rows // rb,),
        in_specs=[blk(past), blk(LANES), _full((PAGE_SIZE, PAGE_ROWS))],
        out_specs=blk(width * N_HEADS),
        out_shape=jax.ShapeDtypeStruct((rows, width * N_HEADS), F32),
        scratch_shapes=[pltpu.VMEM((rb, width), I32)],
        compiler_params=_params("parallel"),
        name="sample_select",
    )(scores, score_new, spread)


def _sample_attend_body(pt_ref, q_ref, sel_ref, seln_ref, kn_ref, vn_ref, *rest):
    g = ATTEND_PAGES
    kp_refs, vp_refs, (o_ref, m_scr, l_scr, acc_scr) = rest[:g], rest[g:2 * g], rest[2 * g:]
    j = pl.program_id(1)
    q = q_ref[0]
    own = (lax.broadcasted_iota(I32, (N_HEADS, PAGE_ROWS), 1) % N_HEADS
           == lax.broadcasted_iota(I32, (N_HEADS, PAGE_ROWS), 0))

    @pl.when(j == 0)
    def _():
        kn = kn_ref[0].astype(BF16).astype(F32)
        logit = jnp.sum(q.astype(F32) * kn, axis=1, keepdims=True)
        on = seln_ref[0][:, 0:1] > 0.0
        m_scr[...] = jnp.where(on, logit, NEG)
        l_scr[...] = jnp.where(on, jnp.ones_like(logit), 0.0)
        acc_scr[...] = jnp.where(on, vn_ref[0].astype(BF16).astype(F32), 0.0)

    sels = sel_ref[0]
    masks, logits = [], []
    for s in range(g):
        mask = jnp.logical_and(own, sels[:, s * PAGE_ROWS:(s + 1) * PAGE_ROWS] > 0.0)
        masks.append(mask)
        logits.append(jnp.where(mask, _dot_nt(q, kp_refs[s][...].astype(BF16)), NEG))
    m_old = m_scr[...]
    m_new = m_old
    for s in range(g):
        m_new = jnp.maximum(m_new, jnp.max(logits[s], axis=1, keepdims=True))
    alpha = jnp.exp2(m_old - m_new)
    l_new = alpha * l_scr[...]
    acc = alpha * acc_scr[...]
    for s in range(g):
        p = jnp.where(masks[s], jnp.exp2(logits[s] - m_new), 0.0)
        l_new = l_new + jnp.sum(p, axis=1, keepdims=True)
        acc = acc + _dot(p.astype(BF16), vp_refs[s][...].astype(BF16))
    m_scr[...] = m_new
    l_scr[...] = l_new
    acc_scr[...] = acc

    @pl.when(j == pl.num_programs(1) - 1)
    def _():
        o_ref[0] = acc_scr[...] / l_scr[...]


def _sample_attend(page_table, qb, sel, k_new, v_new, cache_k, cache_v, n_seq, n_pages):
    g = ATTEND_PAGES
    n_pool = cache_k.shape[0]
    sel3 = sel.reshape(n_seq, 1, (n_pages + 1) * PAGE_ROWS)
    head = lambda: pl.BlockSpec((1, N_HEADS, HEAD_DIM), lambda b, j, pt: (b, 0, 0))
    page = lambda s: pl.BlockSpec((PAGE_ROWS, HEAD_DIM), lambda b, j, pt: (pt[b * n_pages + j * g + s], 0))
    as_rows = lambda c: c.reshape(n_pool * PAGE_ROWS, HEAD_DIM)
    grid_spec = pltpu.PrefetchScalarGridSpec(
        num_scalar_prefetch=1,
        grid=(n_seq, n_pages // g),
        in_specs=[head(),
                  pl.BlockSpec((1, 1, g * PAGE_ROWS), lambda b, j, pt: (b, 0, j)),
                  pl.BlockSpec((1, 1, PAGE_ROWS), lambda b, j, pt: (b, 0, n_pages)),
                  head(), head()] + [page(s) for s in range(g)] * 2,
        out_specs=head(),
        scratch_shapes=[pltpu.VMEM((N_HEADS, 1), F32), pltpu.VMEM((N_HEADS, 1), F32),
                        pltpu.VMEM((N_HEADS, HEAD_DIM), F32)])
    to_heads = lambda z: z.reshape(n_seq, N_HEADS, HEAD_DIM)
    out = pl.pallas_call(
        _sample_attend_body,
        grid_spec=grid_spec,
        out_shape=jax.ShapeDtypeStruct((n_seq, N_HEADS, HEAD_DIM), F32),
        compiler_params=_params("parallel", "arbitrary"),
        name="sample_attend",
    )(page_table.reshape(-1), to_heads(qb), sel3, sel3, to_heads(k_new), to_heads(v_new),
      *([as_rows(cache_k)] * g), *([as_rows(cache_v)] * g))
    return out.reshape(n_seq, D_ATT)


def _layer_norm(x, g, b):
    mu = jnp.mean(x, axis=-1, keepdims=True)
    d = x - mu
    var = jnp.mean(d * d, axis=-1, keepdims=True)
    return d * lax.rsqrt(var + LN_EPS) * g + b


def _post_body(x_ref, orw_ref, oatt_ref, wo_ref, g_ref, b_ref, wr_ref, h_o, gate_o):
    mix = _dot(orw_ref[...].astype(BF16), wo_ref[0:D_RWKV, :]) + _dot(oatt_ref[...].astype(BF16), wo_ref[D_RWKV:, :])
    h = _layer_norm(ALPHA * x_ref[...] + mix, g_ref[...], b_ref[...])
    h_o[...] = h
    logit = _dot_hi(h, wr_ref[...])
    rows = logit.shape[0]
    lane = lax.broadcasted_iota(I32, (rows, LANES), 1).astype(F32)
    big = float(LANES)

    def first_max(mask):
        val = jnp.max(jnp.where(mask, logit, -jnp.inf), axis=1, keepdims=True)
        idx = jnp.min(jnp.where(jnp.logical_and(mask, logit == val), lane, big), axis=1, keepdims=True)
        return val, idx

    is_grp = jnp.logical_and(lane >= N_EXPERTS, lane < N_EXPERTS + N_GROUPS)
    g_max, g_lane = first_max(is_grp)
    p_grp = 1.0 / jnp.sum(jnp.where(is_grp, jnp.exp(logit - g_max), 0.0), axis=1, keepdims=True)
    in_grp = jnp.floor(lane * (1.0 / EXPERTS_PER_GROUP)) == (g_lane - N_EXPERTS)
    v1, i1 = first_max(in_grp)
    v2, i2 = first_max(jnp.logical_and(in_grp, lane != i1))
    e2 = jnp.exp(v2 - v1)
    g1 = p_grp / (1.0 + e2)
    g2 = p_grp * e2 / (1.0 + e2)
    gate_o[...] = jnp.where(lane == i1, g1, 0.0) + jnp.where(lane == i2, g2, 0.0)


def _post(x, o_rw, o_att, pw, tm):
    n = x.shape[0]
    row = lambda w: pl.BlockSpec((tm, w), lambda i: (i, 0))
    return pl.pallas_call(
        _post_body,
        grid=(n // tm,),
        in_specs=[row(D_MODEL), row(D_RWKV), row(D_ATT), _full((D_MODEL, D_MODEL)), _full((1, D_MODEL)),
                  _full((1, D_MODEL)), _full((D_MODEL, LANES))],
        out_specs=[row(D_MODEL), row(LANES)],
        out_shape=[jax.ShapeDtypeStruct((n, D_MODEL), F32), jax.ShapeDtypeStruct((n, LANES), F32)],
        compiler_params=_params("parallel"),
        name="post",
    )(x, o_rw, o_att, pw["wout"], pw["ln1g"], pw["ln1b"], pw["wrouter"])


def _moe_body(h_ref, gate_ref, w1_ref, w3_ref, w2_ref, g_ref, b_ref, y_ref, hb_scr, acc_scr):
    e = pl.program_id(1)

    @pl.when(e == 0)
    def _():
        hb_scr[...] = h_ref[...].astype(BF16)
        acc_scr[...] = jnp.zeros_like(acc_scr)

    hb = hb_scr[...]
    gates = gate_ref[0]
    for j in range(EXPERT_CHUNK):
        h1 = _dot(hb, w1_ref[j])
        h3 = _dot(hb, w3_ref[j])
        hidden = jax.nn.silu(h1) * h3 * gates[:, j:j + 1]
        acc_scr[...] += _dot(hidden.astype(BF16), w2_ref[j])

    @pl.when(e == pl.num_programs(1) - 1)
    def _():
        y_ref[...] = _layer_norm(ALPHA * h_ref[...] + acc_scr[...], g_ref[...], b_ref[...])


def _moe(h, gate, pw, tm):
    n = h.shape[0]
    n_ec = N_EXPERTS // EXPERT_CHUNK
    gate_c = gate[:, :N_EXPERTS].reshape(n, n_ec, EXPERT_CHUNK).transpose(1, 0, 2)
    return pl.pallas_call(
        _moe_body,
        grid=(n // tm, n_ec),
        in_specs=[pl.BlockSpec((tm, D_MODEL), lambda i, e: (i, 0)),
                  pl.BlockSpec((1, tm, EXPERT_CHUNK), lambda i, e: (e, i, 0)),
                  pl.BlockSpec((EXPERT_CHUNK, D_MODEL, D_EXPERT), lambda i, e: (e, 0, 0)),
                  pl.BlockSpec((EXPERT_CHUNK, D_MODEL, D_EXPERT), lambda i, e: (e, 0, 0)),
                  pl.BlockSpec((EXPERT_CHUNK, D_EXPERT, D_MODEL), lambda i, e: (e, 0, 0)),
                  _full((1, D_MODEL)), _full((1, D_MODEL))],
        out_specs=pl.BlockSpec((tm, D_MODEL), lambda i, e: (i, 0)),
        out_shape=jax.ShapeDtypeStruct((n, D_MODEL), F32),
        scratch_shapes=[pltpu.VMEM((tm, D_MODEL), BF16), pltpu.VMEM((tm, D_MODEL), F32)],
        compiler_params=_params("parallel", "arbitrary"),
        name="moe",
    )(h, gate_c, pw["w1"], pw["w3"], pw["w2"], pw["ln2g"], pw["ln2b"])


def _prepare_weights(w_in, rw_mu, rw_w0, rw_w_up, rw_a0, rw_a_up, rw_g_up, rw_k_k, rw_k_a, rw_r_k,
                     rw_ln_g, rw_ln_b, idx_ln_g, idx_ln_b, w_out, ln1_g, ln1_b,
                     moe_w_grp, moe_w_exp, moe_w1, moe_w3, moe_w2, ln2_g, ln2_b):
    row = lambda z: z.reshape(1, -1).astype(F32)
    c5 = RW_COLS + 4 * D_ATT
    pad_to = lambda z, w: jnp.pad(z, ((0, 0), (0, w - z.shape[1])))
    wtail = jnp.concatenate([pad_to(w_in[:, c5:c5 + IDX_DIM], LANES),
                             pad_to(w_in[:, c5 + IDX_DIM:], LANES)], axis=1)
    zeros_w = jnp.zeros((LORA_W, D_RWKV), F32)
    return {
        "wrw": w_in[:, :RW_COLS].astype(BF16),
        "watt": w_in[:, RW_COLS:c5].astype(BF16),
        "wtail": wtail.astype(BF16),
        "mu": row(rw_mu), "w0": row(rw_w0), "a0": row(rw_a0), "kk": row(rw_k_k), "ka": row(rw_k_a),
        "wup": jnp.concatenate([rw_w_up, zeros_w], 0).astype(BF16),
        "aup": jnp.concatenate([zeros_w, rw_a_up], 0).astype(BF16),
        "gup": rw_g_up.astype(BF16),
        "lng": pad_to(row(idx_ln_g), LANES), "lnb": pad_to(row(idx_ln_b), LANES),
        "rk": row(rw_r_k), "rwlng": row(rw_ln_g), "rwlnb": row(rw_ln_b),
        "wout": w_out.astype(BF16), "ln1g": row(ln1_g), "ln1b": row(ln1_b),
        "wrouter": pad_to(jnp.concatenate([moe_w_exp, moe_w_grp], 1).astype(F32), LANES),
        "w1": moe_w1.astype(BF16), "w3": moe_w3.astype(BF16), "w2": moe_w2.astype(BF16),
        "ln2g": row(ln2_g), "ln2b": row(ln2_b),
    }


def _row_tile(n, want):
    return min(n, want)


def _layer(x, x_shift, pos_tables, pw, attend, scan, *, tm_proj, tm_post, tm_moe):
    n = x.shape[0]
    outs = _proj(x, x_shift, pos_tables, pw, _row_tile(n, tm_proj))
    rw = outs[:7]
    qb, kf, vf, kb, vt, qib, kif, kib, wi = outs[7:]
    o_rw, s_fin = scan(rw)
    o_att = attend(qb, kf, vf, kb, vt, qib, kif, kib, wi)
    h, gate = _post(x, o_rw, o_att, pw, _row_tile(n, tm_post))
    y = _moe(h, gate, pw, _row_tile(n, tm_moe))
    return y, kf, vf, kif, s_fin


def kernel(x_prompt, x_sample, cache_k, cache_v, cache_kidx, state_wkv, state_shift, page_table, w_in, rw_mu, rw_w0, rw_w_up, rw_a0, rw_a_up, rw_g_up, rw_k_k, rw_k_a, rw_r_k, rw_ln_g, rw_ln_b, idx_ln_g, idx_ln_b, w_out, ln1_g, ln1_b, moe_w_grp, moe_w_exp, moe_w1, moe_w3, moe_w2, ln2_g, ln2_b):
    pw = _prepare_weights(w_in, rw_mu, rw_w0, rw_w_up, rw_a0, rw_a_up, rw_g_up, rw_k_k, rw_k_a, rw_r_k,
                          rw_ln_g, rw_ln_b, idx_ln_g, idx_ln_b, w_out, ln1_g, ln1_b,
                          moe_w_grp, moe_w_exp, moe_w1, moe_w3, moe_w2, ln2_g, ln2_b)
    bp, tp, _ = x_prompt.shape
    bs, ts, _ = x_sample.shape
    assert ts == 1
    n_pages = page_table.shape[1]
    past = n_pages * PAGE_SIZE
    assert n_pages % SCORE_PAGES == 0 and n_pages % ATTEND_PAGES == 0

    xp = x_prompt.reshape(bp * tp, D_MODEL)
    xp_shift = jnp.concatenate([jnp.zeros((bp, 1, D_MODEL), x_prompt.dtype), x_prompt[:, :-1]], axis=1)

    def attend_p(qb, kf, vf, kb, vt, qib, kif, kib, wi):
        return _dsa_prompt(qib, wi.T, qb, kib, kb, vt, bp, tp)

    def scan_p(rw):
        s0 = jnp.zeros((bp, N_HEADS, HEAD_DIM, HEAD_DIM), F32)
        return _scan_chunked(rw, s0, pw["rk"], pw["rwlng"], pw["rwlnb"], bp, tp)

    y_p, k_p, v_p, ki_p, wkv_p = _layer(xp, xp_shift.reshape(bp * tp, D_MODEL), _rope_tables(jnp.arange(tp)),
                                        pw, attend_p, scan_p, tm_proj=256, tm_post=512, tm_moe=1024)

    xs = x_sample.reshape(bs, D_MODEL)

    def attend_s(qb, kf, vf, kb, vt, qib, kif, kib, wi):
        scores, score_new = _sample_scores(page_table, qib, wi, kif.astype(cache_kidx.dtype), cache_kidx, bs, n_pages)
        sel = _sample_select(scores.reshape(bs, past), score_new.reshape(bs, LANES), past)
        return _sample_attend(page_table, qb, sel, kf.astype(cache_k.dtype), vf.astype(cache_v.dtype),
                              cache_k, cache_v, bs, n_pages)

    def scan_s(rw):
        return _scan_step(rw, state_wkv.astype(F32), pw["rk"], pw["rwlng"], pw["rwlnb"], bs)

    y_s, k_s, v_s, ki_s, wkv_s = _layer(xs, state_shift.astype(x_sample.dtype),
                                        _rope_tables(jnp.full((bs,), past, I32)),
                                        pw, attend_s, scan_s, tm_proj=256, tm_post=512, tm_moe=1024)

    heads = lambda z, b, t: z.reshape(b, t, N_HEADS, HEAD_DIM)
    return (y_p.reshape(bp, tp, D_MODEL), y_s.reshape(bs, ts, D_MODEL),
            heads(k_p, bp, tp).astype(cache_k.dtype), heads(v_p, bp, tp).astype(cache_v.dtype),
            ki_p.reshape(bp, tp, IDX_DIM).astype(cache_kidx.dtype),
            wkv_p.astype(state_wkv.dtype), x_prompt[:, -1].astype(state_shift.dtype),
            heads(k_s, bs, ts).astype(cache_k.dtype), heads(v_s, bs, ts).astype(cache_v.dtype),
            ki_s.reshape(bs, ts, IDX_DIM).astype(cache_kidx.dtype),
            wkv_s.astype(state_wkv.dtype), x_sample[:, -1].astype(state_shift.dtype))
```

```python
import functools
import math

import jax
import jax.numpy as jnp
from jax import lax
from jax.experimental import pallas as pl
from jax.experimental.pallas import tpu as pltpu

F32 = jnp.float32
BF16 = jnp.bfloat16
I32 = jnp.int32

D_MODEL = 1024
HEAD_DIM = 64
D_RWKV = 512
D_ATT = 512
N_HEADS = 8
LORA_W = 64
LORA_A = 64
LORA_G = 128
N_IDX_HEADS = 8
IDX_DIM = 64
TOPK_MAX = 256
ROPE_THETA = 500000.0
PAGE_SIZE = 128
N_GROUPS = 4
EXPERTS_PER_GROUP = 8
N_EXPERTS = 32
D_EXPERT = 256
LN_EPS = 1e-5
GN_EPS = HEAD_DIM * 1e-5
DECAY_SCALE = math.exp(-0.5)
ALPHA = 2.0 ** 0.25
NEG = -1e30
RW_COLS = 3 * D_RWKV + LORA_W + LORA_A + LORA_G
WI_SCALE = N_IDX_HEADS ** -0.5 * IDX_DIM ** -0.5
QK_SCALE = HEAD_DIM ** -0.5 * math.log2(math.e)
INT_MIN = -(2 ** 31)
INT_MAX = 2 ** 31 - 1

LANES = 128
SUBLANES = 8
VMEM_LIMIT = 56 * 1024 * 1024
SCAN_CHUNK = 64
SOLVE_BLOCK = 16
EXPERT_CHUNK = 4
DSA_QUERIES = 512
DSA_KEYS = 256
ACC_ROWS = HEAD_DIM + 16
SCORE_PAGES = 8
ATTEND_PAGES = 4
PAGE_ROWS = PAGE_SIZE * N_HEADS


def _dot(a, b):
    return jnp.dot(a, b, preferred_element_type=F32)


def _dot_nt(a, b):
    return lax.dot_general(a, b, (((1,), (1,)), ((), ())), preferred_element_type=F32)


def _dot_hi(a, b):
    return jnp.dot(a, b, preferred_element_type=F32, precision=lax.Precision.HIGHEST)


def _split(x):
    hi = x.astype(BF16)
    return hi, (x - hi.astype(F32)).astype(BF16)


def _dot3(a, b, nt=False):
    f = _dot_nt if nt else _dot
    ah, al = _split(a)
    bh, bl = _split(b)
    return f(ah, bh) + (f(ah, bl) + f(al, bh))


def _params(*sem):
    return pltpu.CompilerParams(dimension_semantics=sem, vmem_limit_bytes=VMEM_LIMIT)


def _full(shape):
    nd = len(shape)
    return pl.BlockSpec(shape, lambda *_: (0,) * nd)


def _rope(x, cc, s1, s2):
    outs = []
    for c in range(x.shape[1] // LANES):
        xc = x[:, c * LANES:(c + 1) * LANES]
        outs.append(xc * cc + pltpu.roll(xc, LANES - 8, 1) * s1 + pltpu.roll(xc, 8, 1) * s2)
    return outs[0] if len(outs) == 1 else jnp.concatenate(outs, axis=1)


def _proj_body(x_ref, *refs, tiles_per_seq):
    if tiles_per_seq:
        (wrw_ref, watt_ref, wtail_ref, mu_ref, w0_ref, wup_ref, a0_ref, aup_ref, gup_ref, kk_ref, ka_ref,
         lng_ref, lnb_ref, cc_ref, s1_ref, s2_ref, r_o, lw_o, k_o, v_o, kkr_o, a_o, g_o, qb_o, kf_o, vf_o, kb_o,
         vt_o, qib_o, kif_o, kib_o, wi_o, carry_scr) = refs
    else:
        (xs_ref, wrw_ref, watt_ref, wtail_ref, mu_ref, w0_ref, wup_ref, a0_ref, aup_ref, gup_ref, kk_ref, ka_ref,
         lng_ref, lnb_ref, cc_ref, s1_ref, s2_ref, r_o, lw_o, k_o, v_o, kkr_o, a_o, g_o, qb_o, kf_o, vf_o, kb_o,
         vt_o, qib_o, kif_o, kib_o, wi_o) = refs
        xsb = xs_ref[...].astype(BF16)
    xb = x_ref[...].astype(BF16)
    tm = x_ref.shape[0]
    if tiles_per_seq:
        first_row = lax.broadcasted_iota(I32, (tm, 1), 0) == 0

        @pl.when(pl.program_id(0) % tiles_per_seq == 0)
        def _():
            carry_scr[...] = jnp.zeros_like(carry_scr)

    def lerp(c0, c1):
        w = wrw_ref[:, c0:c1]
        cur = _dot(xb, w)
        if tiles_per_seq:
            prev = jnp.where(first_row, carry_scr[0:1, c0:c1], pltpu.roll(cur, 1, 0))
            carry_scr[0:1, c0:c1] = cur[tm - 1:tm, :]
        else:
            prev = _dot(xsb, w)
        return cur + mu_ref[:, c0:c1] * (prev - cur)

    r_o[...] = lerp(0, D_RWKV)
    k = lerp(D_RWKV, 2 * D_RWKV)
    v_o[...] = lerp(2 * D_RWKV, 3 * D_RWKV)
    lora = lerp(3 * D_RWKV, 3 * D_RWKV + LORA_W + LORA_A)
    dg = lerp(3 * D_RWKV + LORA_W + LORA_A, RW_COLS)
    lw_o[...] = -DECAY_SCALE * jax.nn.sigmoid(w0_ref[...] + _dot(jnp.tanh(lora).astype(BF16), wup_ref[...]))
    a = jax.nn.sigmoid(a0_ref[...] + _dot(lora.astype(BF16), aup_ref[...]))
    a_o[...] = a
    g_o[...] = _dot(jax.nn.sigmoid(dg).astype(BF16), gup_ref[...])
    kkr_o[...] = k * kk_ref[...]
    k_o[...] = k * (1.0 + (a - 1.0) * ka_ref[...])

    cc, s1, s2 = cc_ref[...], s1_ref[...], s2_ref[...]
    q = _rope(_dot(xb, watt_ref[:, 0:D_ATT]), cc, s1, s2)
    qb_o[...] = (q * QK_SCALE).astype(BF16)
    ka = _rope(_dot(xb, watt_ref[:, D_ATT:2 * D_ATT]), cc, s1, s2)
    kf_o[...] = ka
    kb_o[...] = ka.astype(BF16)
    va = _dot(xb, watt_ref[:, 2 * D_ATT:3 * D_ATT])
    vf_o[...] = va
    vt_o[...] = va.T.astype(BF16)
    qi = _rope(_dot(xb, watt_ref[:, 3 * D_ATT:4 * D_ATT]), cc, s1, s2)
    qib_o[...] = qi.astype(BF16)

    tail = _dot(xb, wtail_ref[...])
    t0 = tail[:, :LANES]
    in_ki = lax.broadcasted_iota(I32, (1, LANES), 1) < IDX_DIM
    mean = jnp.sum(jnp.where(in_ki, t0, 0.0), axis=1, keepdims=True) * (1.0 / IDX_DIM)
    d = jnp.where(in_ki, t0 - mean, 0.0)
    var = jnp.sum(d * d, axis=1, keepdims=True) * (1.0 / IDX_DIM)
    ki = _rope(d * lax.rsqrt(var + LN_EPS) * lng_ref[...] + lnb_ref[...], cc, s1, s2)[:, :IDX_DIM]
    kif_o[...] = ki
    kib_o[...] = ki.astype(BF16)
    wi_o[...] = tail[:, LANES:LANES + N_IDX_HEADS] * WI_SCALE


def _rope_tables(pos):
    half = HEAD_DIM // 8
    inv = ROPE_THETA ** (-jnp.arange(half, dtype=F32) / half)
    ang = pos.astype(F32)[:, None] * inv[None, :]
    cos, sin = jnp.cos(ang), jnp.sin(ang)
    n = pos.shape[0]
    one = jnp.ones((n, HEAD_DIM - 2 * half), F32)
    zero = jnp.zeros((n, HEAD_DIM - 2 * half), F32)
    zh = jnp.zeros((n, half), F32)
    cc = jnp.concatenate([cos, cos, one], 1)
    s1 = jnp.concatenate([-sin, zh, zero], 1)
    s2 = jnp.concatenate([zh, sin, zero], 1)
    return tuple(jnp.tile(t, (1, LANES // HEAD_DIM)) for t in (cc, s1, s2))


def _proj(x, xs, tables, pw, tm):
    n = x.shape[0]
    nt = tables[0].shape[0] // tm
    tiles_per_seq = nt if xs is None else 0
    row = lambda w: pl.BlockSpec((tm, w), lambda i: (i, 0))
    tab = pl.BlockSpec((tm, LANES), lambda i: (i % nt, 0))
    weights = (pw["wrw"], pw["watt"], pw["wtail"], pw["mu"], pw["w0"], pw["wup"], pw["a0"], pw["aup"],
               pw["gup"], pw["kk"], pw["ka"], pw["lng"], pw["lnb"])
    f = lambda w, dt=F32: jax.ShapeDtypeStruct((n, w), dt)
    out_shape = ([f(D_RWKV)] * 7 + [f(D_ATT, BF16), f(D_ATT), f(D_ATT), f(D_ATT, BF16),
                                    jax.ShapeDtypeStruct((D_ATT, n), BF16),
                                    f(D_ATT, BF16), f(IDX_DIM), f(IDX_DIM, BF16), f(N_IDX_HEADS)])
    out_specs = ([row(D_RWKV)] * 7 + [row(D_ATT)] * 4 + [pl.BlockSpec((D_ATT, tm), lambda i: (0, i))]
                 + [row(D_ATT), row(IDX_DIM), row(IDX_DIM), row(N_IDX_HEADS)])
    rows_in = (x,) if xs is None else (x, xs)
    return pl.pallas_call(
        functools.partial(_proj_body, tiles_per_seq=tiles_per_seq),
        grid=(n // tm,),
        in_specs=[row(D_MODEL)] * len(rows_in) + [_full(w.shape) for w in weights] + [tab] * 3,
        out_specs=out_specs,
        out_shape=out_shape,
        scratch_shapes=[pltpu.VMEM((SUBLANES, RW_COLS), F32)] if xs is None else [],
        compiler_params=_params("arbitrary" if xs is None else "parallel"),
        name="proj",
    )(*rows_in, *weights, *tables)


def _head_epilogue(o, r, k, v, g, rk, lng, lnb):
    mu = jnp.mean(o, axis=-1, keepdims=True)
    d = o - mu
    var = jnp.mean(d * d, axis=-1, keepdims=True)
    on = d * lax.rsqrt(var + GN_EPS) * lng + lnb
    bonus = jnp.sum(r * k * rk, axis=-1, keepdims=True) * v
    return (on + bonus) * g


def _scan_chunk_body(r_ref, lw_ref, k_ref, v_ref, kk_ref, a_ref, g_ref, rk_ref, lng_ref, lnb_ref, s0_ref,
                     o_ref, sf_ref, s_scr):
    c = pl.program_id(1)

    @pl.when(c == 0)
    def _():
        s_scr[...] = s0_ref[0]

    C = r_ref.shape[0]
    heads = range(N_HEADS)
    row = lax.broadcasted_iota(I32, (C, 2 * C), 0)
    col = lax.broadcasted_iota(I32, (C, 2 * C), 1) % C
    strict2 = col < row
    incl2 = col <= row
    same_blk = (col // SOLVE_BLOCK) == (row // SOLVE_BLOCK)
    left = lax.broadcasted_iota(I32, (C, 2 * C), 1) < C
    tri = incl2[:, :C].astype(BF16)

    lw_all = lw_ref[...]
    l_hi = lw_all.astype(BF16)
    l_r1 = lw_all - l_hi.astype(F32)
    l_mid = l_r1.astype(BF16)
    l_lo = (l_r1 - l_mid.astype(F32)).astype(BF16)
    cum_all = _dot(tri, l_hi) + (_dot(tri, l_mid) + _dot(tri, l_lo))

    sls = [slice(h * HEAD_DIM, (h + 1) * HEAD_DIM) for h in heads]
    r = [r_ref[:, sl] for sl in sls]
    k = [k_ref[:, sl] for sl in sls]
    v = [v_ref[:, sl] for sl in sls]
    lhs1, rhs1, bk_end, g_end = [], [], [], []
    for h in heads:
        sl = sls[h]
        lw, kkr, a = lw_all[:, sl], kk_ref[:, sl], a_ref[:, sl]
        cum = cum_all[:, sl]
        cum_end = cum[C - 1:C, :]
        kk = kkr * lax.rsqrt(jnp.maximum(jnp.sum(kkr * kkr, axis=-1, keepdims=True), 1e-24))
        beta = kk * a
        e_neg = jnp.exp(-cum)
        e_end = jnp.exp(cum_end - cum)
        lhs1.append(jnp.concatenate([-kk * jnp.exp(cum - lw), r[h] * jnp.exp(cum)], axis=0))
        rhs1.append(jnp.concatenate([beta * e_neg, k[h] * e_neg, s_scr[h]], axis=0))
        bk_end.append(jnp.concatenate([beta * e_end, k[h] * e_end], axis=0))
        g_end.append(jnp.exp(cum_end))

    r1 = [_dot3(lhs1[h], rhs1[h], nt=True) for h in heads]
    vv = [jnp.concatenate([v[h], v[h]], axis=0) for h in heads]
    y = [r1[h][:C, 2 * C:] + _dot3(jnp.where(jnp.logical_and(strict2, ~left), r1[h][:C, :2 * C], 0.0), vv[h])
         for h in heads]
    n_pair = [r1[h][:C, :2 * C] for h in heads]
    p = [jnp.where(jnp.logical_and(strict2, same_blk), n_pair[h], 0.0)[:, :C] for h in heads]
    x = [jnp.where(left, jnp.where(jnp.logical_and(strict2, ~same_blk), n_pair[h], 0.0),
                   jnp.concatenate([y[h], y[h]], axis=1)) for h in heads]
    n_steps = SOLVE_BLOCK.bit_length() - 1
    for i in range(n_steps):
        if i + 1 < n_steps:
            rr = [_dot3(p[h], jnp.concatenate([x[h], p[h]], axis=1)) for h in heads]
            x = [x[h] + rr[h][:, :2 * C] for h in heads]
            p = [rr[h][:, 2 * C:] for h in heads]
        else:
            x = [x[h] + _dot3(p[h], x[h]) for h in heads]
    rr = [_dot3(x[h][:, :C], x[h]) for h in heads]
    x = [jnp.where(left, rr[h], x[h] + rr[h]) for h in heads]
    rr = [_dot3(x[h][:, :C], x[h]) for h in heads]
    u = [(x[h] + rr[h])[:, C:] for h in heads]

    uv = [jnp.concatenate([u[h], v[h]], axis=0) for h in heads]
    o = [r1[h][C:, 2 * C:] + _dot3(jnp.where(incl2, r1[h][C:, :2 * C], 0.0), uv[h]) for h in heads]
    for h in heads:
        s_new = rhs1[h][2 * C:, :] * g_end[h] + _dot3(uv[h].T, bk_end[h])
        s_scr[h] = s_new
        sl = sls[h]
        o_ref[:, sl] = _head_epilogue(o[h], r[h], k[h], v[h], g_ref[:, sl], rk_ref[:, sl], lng_ref[:, sl],
                                      lnb_ref[:, sl])

    @pl.when(c == pl.num_programs(1) - 1)
    def _():
        sf_ref[0] = s_scr[...]


def _scan_chunked(rw, s0, rk, lng, lnb, n_seq, t):
    nc = t // SCAN_CHUNK
    row = pl.BlockSpec((SCAN_CHUNK, D_RWKV), lambda b, c: (b * nc + c, 0))
    st = pl.BlockSpec((1, N_HEADS, HEAD_DIM, HEAD_DIM), lambda b, c: (b, 0, 0, 0))
    par = _full((1, D_RWKV))
    return pl.pallas_call(
        _scan_chunk_body,
        grid=(n_seq, nc),
        in_specs=[row] * 7 + [par] * 3 + [st],
        out_specs=[row, st],
        out_shape=[jax.ShapeDtypeStruct((n_seq * t, D_RWKV), F32),
                   jax.ShapeDtypeStruct((n_seq, N_HEADS, HEAD_DIM, HEAD_DIM), F32)],
        scratch_shapes=[pltpu.VMEM((N_HEADS, HEAD_DIM, HEAD_DIM), F32)],
        compiler_params=_params("parallel", "arbitrary"),
        name="scan_chunked",
    )(*rw, rk, lng, lnb, s0)


def _scan_step_body(r_ref, lw_ref, k_ref, v_ref, kk_ref, a_ref, g_ref, rk_ref, lng_ref, lnb_ref, s0_ref,
                    o_ref, sf_ref):
    eye = (lax.broadcasted_iota(I32, (HEAD_DIM, HEAD_DIM), 0)
           == lax.broadcasted_iota(I32, (HEAD_DIM, HEAD_DIM), 1)).astype(F32)
    outs = []
    for h in range(N_HEADS):
        sl = slice(h * HEAD_DIM, (h + 1) * HEAD_DIM)
        r, lw, k, v = r_ref[0, :, sl], lw_ref[0, :, sl], k_ref[0, :, sl], v_ref[0, :, sl]
        kkr, a = kk_ref[0, :, sl], a_ref[0, :, sl]
        kk = kkr * lax.rsqrt(jnp.maximum(jnp.sum(kkr * kkr, axis=-1, keepdims=True), 1e-24))
        s = s0_ref[0, h]
        sa = jnp.sum(s * kk, axis=1, keepdims=True)
        v_col = jnp.sum(eye * v, axis=1, keepdims=True)
        s = s * jnp.exp(lw) - sa * (kk * a) + v_col * k
        sf_ref[0, h] = s
        o_col = jnp.sum(s * r, axis=1, keepdims=True)
        o = jnp.sum(eye * o_col, axis=0, keepdims=True)
        outs.append(_head_epilogue(o, r, k, v, g_ref[0, :, sl], rk_ref[:, sl], lng_ref[:, sl], lnb_ref[:, sl]))
    o_ref[0] = jnp.concatenate(outs, axis=1)


def _scan_step(rw, s0, rk, lng, lnb, n_seq):
    row = pl.BlockSpec((1, 1, D_RWKV), lambda b: (b, 0, 0))
    st = pl.BlockSpec((1, N_HEADS, HEAD_DIM, HEAD_DIM), lambda b: (b, 0, 0, 0))
    par = _full((1, D_RWKV))
    rw3 = [z.reshape(n_seq, 1, D_RWKV) for z in rw]
    o, sf = pl.pallas_call(
        _scan_step_body,
        grid=(n_seq,),
        in_specs=[row] * 7 + [par] * 3 + [st],
        out_specs=[row, st],
        out_shape=[jax.ShapeDtypeStruct((n_seq, 1, D_RWKV), F32),
                   jax.ShapeDtypeStruct((n_seq, N_HEADS, HEAD_DIM, HEAD_DIM), F32)],
        compiler_params=_params("parallel"),
        name="scan_step",
    )(*rw3, rk, lng, lnb, s0)
    return o.reshape(n_seq, D_RWKV), sf


def _order_key(score):
    bits = pltpu.bitcast(score, I32)
    return bits ^ ((bits >> 31) & INT_MAX)


def _select_threshold(count_ge, count_tie_le, n_valid, ktop, idx_bits):
    shape = n_valid.shape
    lo = jnp.full(shape, INT_MIN + 1, I32)
    hi = jnp.where(n_valid <= ktop, lo + 1, INT_MAX)

    def cond(st):
        it, lo, hi, _ = st
        return jnp.logical_and(it < 33, jnp.max(jnp.where(lo + 1 != hi, 1.0, 0.0)) > 0.0)

    def body(st):
        it, lo, hi, c_lo = st
        mid = (lo & hi) + ((lo ^ hi) >> 1)
        cnt = count_ge(mid)
        live = lo + 1 != hi
        up = jnp.logical_and(live, cnt >= ktop)
        down = jnp.logical_and(live, cnt < ktop)
        hi2 = jnp.where(jnp.logical_and(up, cnt == ktop), mid + 1, jnp.where(down, mid, hi))
        return it + 1, jnp.where(up, mid, lo), hi2, jnp.where(up, cnt, c_lo)

    _, thr, _, c_thr = lax.while_loop(cond, body, (jnp.int32(0), lo, hi, n_valid.astype(F32)))
    excess = jnp.logical_and(c_thr > ktop, n_valid > ktop)
    has_tie = jnp.max(jnp.where(excess, 1.0, 0.0)) > 0.0
    all_idx = jnp.full(shape, (1 << idx_bits) - 1, I32)

    def tie_search():
        need = ktop - count_ge(thr + 1)

        def tbody(_, st):
            lo_j, hi_j = st
            mid = (lo_j + hi_j) >> 1
            ok = count_tie_le(thr, mid) >= need
            return jnp.where(ok, lo_j, mid + 1), jnp.where(ok, mid, hi_j)

        lo_j, _ = lax.fori_loop(0, idx_bits, tbody, (jnp.zeros(shape, I32), all_idx))
        return jnp.where(excess, lo_j, all_idx)

    tie_end = lax.cond(has_tie, tie_search, lambda: all_idx)
    return thr, tie_end, has_tie


def _dsa_prompt_body(qi_ref, wit_ref, q_ref, ki_ref, k_ref, vt_ref, o_ref, key_scr, m_scr, bias_scr, s_scr, acc_scr,
                     *, ktop, tq, tk, idx_bits):
    i = pl.program_id(1)
    n_kt = (i + 1) * (tq // tk)
    qpos = i * tq + lax.broadcasted_iota(I32, (1, tq), 1)
    key_off = lax.broadcasted_iota(I32, (tk, 1), 0)
    tile = lambda kt: pl.ds(pl.multiple_of(kt * tk, tk), tk)
    qi = qi_ref[...]
    qi_h = [qi[:, h * IDX_DIM:(h + 1) * IDX_DIM] for h in range(N_IDX_HEADS)]
    wit = wit_ref[...]

    def score_tile(kt, carry):
        kis = ki_ref[tile(kt), :]
        acc = jnp.zeros((tk, tq), F32)
        for h in range(N_IDX_HEADS):
            acc = acc + jnp.maximum(_dot_nt(kis, qi_h[h]), 0.0) * wit[h:h + 1, :]
        key_scr[tile(kt), :] = jnp.where(kt * tk + key_off <= qpos, _order_key(acc), INT_MIN)
        return carry

    lax.fori_loop(0, n_kt, score_tile, 0)

    def count(pred):
        def body(kt, acc):
            hit = jnp.where(pred(key_scr[tile(kt), :], kt * tk + key_off), 1.0, 0.0)
            return acc + jnp.sum(hit.reshape(tk // SUBLANES, SUBLANES, tq), axis=0)
        acc = lax.fori_loop(0, n_kt, body, jnp.zeros((SUBLANES, tq), F32))
        return jnp.sum(acc, axis=0, keepdims=True)

    thr, tie_end, has_tie = _select_threshold(
        lambda t: count(lambda keys, pos: keys >= t),
        lambda t, j: count(lambda keys, pos: jnp.logical_and(keys == t, pos <= j)),
        qpos + 1, ktop, idx_bits)

    @pl.when(has_tie)
    def _():
        def demote(kt, carry):
            keys = key_scr[tile(kt), :]
            late = jnp.logical_and(keys == thr, kt * tk + key_off > tie_end)
            key_scr[tile(kt), :] = jnp.where(late, thr - 1, keys)
            return carry
        lax.fori_loop(0, n_kt, demote, 0)

    q = q_ref[...]
    lane = lax.broadcasted_iota(I32, (1, LANES), 1)
    q_pad = []
    for h in range(N_HEADS):
        own = (lane < HEAD_DIM) if h % 2 == 0 else (lane >= HEAD_DIM)
        q_pad.append(jnp.where(own, q[:, (h // 2) * LANES:(h // 2 + 1) * LANES].astype(F32), 0.0).astype(BF16))
    m_scr[...] = jnp.full(m_scr.shape, NEG, F32)
    acc_scr[...] = jnp.zeros(acc_scr.shape, F32)
    ones = jnp.ones((ACC_ROWS - HEAD_DIM, tk), BF16)

    def attend(kt, carry):
        bias_scr[...] = jnp.where(key_scr[tile(kt), :] >= thr, 0.0, NEG)
        tile_max = []
        for h in range(N_HEADS):
            s = _dot_nt(k_ref[tile(kt), (h // 2) * LANES:(h // 2 + 1) * LANES], q_pad[h]) + bias_scr[...]
            s_scr[h] = s
            tile_max.append(jnp.max(s, axis=0, keepdims=True))
        m_old = m_scr[...]
        m_new = jnp.maximum(m_old, jnp.concatenate(tile_max, axis=0))
        alpha = jnp.exp2(m_old - m_new)
        m_scr[...] = m_new
        for h in range(N_HEADS):
            p = jnp.exp2(s_scr[h] - m_new[h:h + 1, :]).astype(BF16)
            v_ext = jnp.concatenate([vt_ref[h * HEAD_DIM:(h + 1) * HEAD_DIM, tile(kt)], ones], axis=0)
            rows = slice(h * ACC_ROWS, (h + 1) * ACC_ROWS)
            acc_scr[rows, :] = alpha[h:h + 1, :] * acc_scr[rows, :] + _dot(v_ext, p)
        return carry

    lax.fori_loop(0, n_kt, attend, 0)
    out_t = jnp.concatenate(
        [acc_scr[h * ACC_ROWS:h * ACC_ROWS + HEAD_DIM, :] / acc_scr[h * ACC_ROWS + HEAD_DIM:h * ACC_ROWS + HEAD_DIM + 1, :]
         for h in range(N_HEADS)], axis=0)
    o_ref[...] = out_t.T


def _dsa_prompt(qib, wit, qb, kib, kb, vt, n_seq, t):
    tq = min(t, DSA_QUERIES)
    tk = min(t, DSA_KEYS)
    nq = t // tq
    ktop = min(TOPK_MAX, t // 4)
    qrow = lambda w: pl.BlockSpec((tq, w), lambda b, i: (b * nq + i, 0))
    seq = lambda w: pl.BlockSpec((t, w), lambda b, i: (b, 0), pipeline_mode=pl.Buffered(1))
    body = functools.partial(_dsa_prompt_body, ktop=ktop, tq=tq, tk=tk, idx_bits=max(1, (t - 1).bit_length()))
    return pl.pallas_call(
        body,
        grid=(n_seq, nq),
        in_specs=[qrow(D_ATT), pl.BlockSpec((N_IDX_HEADS, tq), lambda b, i: (0, b * nq + i)), qrow(D_ATT),
                  seq(IDX_DIM), seq(D_ATT),
                  pl.BlockSpec((D_ATT, t), lambda b, i: (0, b), pipeline_mode=pl.Buffered(1))],
        out_specs=qrow(D_ATT),
        out_shape=jax.ShapeDtypeStruct((n_seq * t, D_ATT), F32),
        scratch_shapes=[pltpu.VMEM((t, tq), I32), pltpu.VMEM((N_HEADS, tq), F32), pltpu.VMEM((tk, tq), F32),
                        pltpu.VMEM((N_HEADS, tk, tq), F32), pltpu.VMEM((N_HEADS * ACC_ROWS, tq), F32)],
        compiler_params=_params("parallel", "arbitrary"),
        name="dsa_prompt",
    )(qib, wit, qb, kib, kb, vt)


def _sample_score_body(pt_ref, qi_ref, wi_ref, kin_ref, *rest):
    page_refs, (sc_ref, scn_ref) = rest[:SCORE_PAGES], rest[SCORE_PAGES:]
    qi = qi_ref[0]
    wi_col = wi_ref[0]
    pages = jnp.concatenate([p[0] for p in page_refs], axis=0).astype(BF16)
    rel = jnp.maximum(_dot_nt(qi, pages), 0.0) * wi_col
    sc_ref[0] = 0.0 + jnp.sum(rel, axis=0, keepdims=True)

    @pl.when(pl.program_id(1) == 0)
    def _():
        kn = kin_ref[0].astype(BF16).astype(F32)
        rel_n = jnp.maximum(jnp.sum(qi.astype(F32) * kn, axis=1, keepdims=True), 0.0) * wi_col
        scn_ref[0] = jnp.broadcast_to(0.0 + jnp.sum(rel_n, axis=0, keepdims=True), (1, LANES))


def _sample_scores(page_table, qib, wi, ki_new, cache_kidx, n_seq, n_pages):
    g = SCORE_PAGES
    page = lambda s: pl.BlockSpec((1, PAGE_SIZE, IDX_DIM), lambda b, j, pt: (pt[b * n_pages + j * g + s], 0, 0))
    grid_spec = pltpu.PrefetchScalarGridSpec(
        num_scalar_prefetch=1,
        grid=(n_seq, n_pages // g),
        in_specs=[pl.BlockSpec((1, N_IDX_HEADS, IDX_DIM), lambda b, j, pt: (b, 0, 0)),
                  pl.BlockSpec((1, N_IDX_HEADS, 1), lambda b, j, pt: (b, 0, 0)),
                  pl.BlockSpec((1, 1, IDX_DIM), lambda b, j, pt: (b, 0, 0))] + [page(s) for s in range(g)],
        out_specs=[pl.BlockSpec((1, 1, g * PAGE_SIZE), lambda b, j, pt: (b, 0, j)),
                   pl.BlockSpec((1, 1, LANES), lambda b, j, pt: (b, 0, 0))])
    return pl.pallas_call(
        _sample_score_body,
        grid_spec=grid_spec,
        out_shape=[jax.ShapeDtypeStruct((n_seq, 1, n_pages * PAGE_SIZE), F32),
                   jax.ShapeDtypeStruct((n_seq, 1, LANES), F32)],
        compiler_params=_params("parallel", "arbitrary"),
        name="sample_scores",
    )(page_table.reshape(-1), qib.reshape(n_seq, N_IDX_HEADS, IDX_DIM), wi.reshape(n_seq, N_IDX_HEADS, 1),
      ki_new.reshape(n_seq, 1, IDX_DIM), *([cache_kidx] * g))


def _sample_select_body(sc_ref, scn_ref, spread_ref, sel_ref, key_scr, *, ktop, past, idx_bits):
    rows = sc_ref.shape[0]
    n_t = past // LANES + 1
    lane_pos = lax.broadcasted_iota(I32, (1, LANES), 1)
    tile = lambda kt: pl.ds(pl.multiple_of(kt * LANES, LANES), LANES)
    key_scr[:, :past] = _order_key(sc_ref[...])
    key_scr[:, past:] = jnp.where(lane_pos == 0, _order_key(scn_ref[...]), INT_MIN)

    def count(pred):
        def body(kt, acc):
            return acc + jnp.where(pred(key_scr[:, tile(kt)], kt * LANES + lane_pos), 1.0, 0.0)
        acc = lax.fori_loop(0, n_t, body, jnp.zeros((rows, LANES), F32))
        return jnp.sum(acc, axis=1, keepdims=True)

    thr, tie_end, _ = _select_threshold(
        lambda t: count(lambda keys, pos: keys >= t),
        lambda t, j: count(lambda keys, pos: jnp.logical_and(keys == t, pos <= j)),
        jnp.full((rows, 1), past + 1, I32), ktop, idx_bits)
    spread = spread_ref[...]

    def write(kt, carry):
        keys = key_scr[:, tile(kt)]
        sel = jnp.logical_or(keys > thr, jnp.logical_and(keys >= thr, kt * LANES + lane_pos <= tie_end))
        wide = pl.ds(pl.multiple_of(kt * PAGE_ROWS, PAGE_ROWS), PAGE_ROWS)
        sel_ref[:, wide] = _dot(jnp.where(sel, 1.0, 0.0).astype(BF16), spread)
        return carry

    lax.fori_loop(0, n_t, write, 0)


def _sample_select(scores, score_new, past):
    rows = scores.shape[0]
    width = past + LANES
    ktop = min(TOPK_MAX, (past + 1) // 4)
    spread = (jnp.arange(PAGE_ROWS)[None, :] // N_HEADS == jnp.arange(PAGE_SIZE)[:, None]).astype(BF16)
    body = functools.partial(_sample_select_body, ktop=ktop, past=past, idx_bits=width.bit_length())
    rb = min(rows, SUBLANES)
    blk = lambda w: pl.BlockSpec((rb, w), lambda i: (i, 0))
    return pl.pallas_call(
        body,
        grid=(rows // rb,),
        in_specs=[blk(past), blk(LANES), _full((PAGE_SIZE, PAGE_ROWS))],
        out_specs=blk(width * N_HEADS),
        out_shape=jax.ShapeDtypeStruct((rows, width * N_HEADS), F32),
        scratch_shapes=[pltpu.VMEM((rb, width), I32)],
        compiler_params=_params("parallel"),
        name="sample_select",
    )(scores, score_new, spread)


def _sample_attend_body(pt_ref, q_ref, sel_ref, seln_ref, kn_ref, vn_ref, *rest):
    g = ATTEND_PAGES
    kp_refs, vp_refs, (o_ref, m_scr, l_scr, acc_scr) = rest[:g], rest[g:2 * g], rest[2 * g:]
    rows_bf16 = lambda ref: ref[0].reshape(PAGE_ROWS, HEAD_DIM).astype(BF16)
    j = pl.program_id(1)
    q = q_ref[0]
    own = (lax.broadcasted_iota(I32, (N_HEADS, PAGE_ROWS), 1) % N_HEADS
           == lax.broadcasted_iota(I32, (N_HEADS, PAGE_ROWS), 0))

    @pl.when(j == 0)
    def _():
        kn = kn_ref[0].astype(BF16).astype(F32)
        logit = jnp.sum(q.astype(F32) * kn, axis=1, keepdims=True)
        on = seln_ref[0][:, 0:1] > 0.0
        m_scr[...] = jnp.where(on, logit, NEG)
        l_scr[...] = jnp.where(on, jnp.ones_like(logit), 0.0)
        acc_scr[...] = jnp.where(on, vn_ref[0].astype(BF16).astype(F32), 0.0)

    sels = sel_ref[0]
    masks, logits = [], []
    for s in range(g):
        mask = jnp.logical_and(own, sels[:, s * PAGE_ROWS:(s + 1) * PAGE_ROWS] > 0.0)
        masks.append(mask)
        logits.append(jnp.where(mask, _dot_nt(q, rows_bf16(kp_refs[s])), NEG))
    m_old = m_scr[...]
    m_new = m_old
    for s in range(g):
        m_new = jnp.maximum(m_new, jnp.max(logits[s], axis=1, keepdims=True))
    alpha = jnp.exp2(m_old - m_new)
    l_new = alpha * l_scr[...]
    acc = alpha * acc_scr[...]
    for s in range(g):
        p = jnp.where(masks[s], jnp.exp2(logits[s] - m_new), 0.0)
        l_new = l_new + jnp.sum(p, axis=1, keepdims=True)
        acc = acc + _dot(p.astype(BF16), rows_bf16(vp_refs[s]))
    m_scr[...] = m_new
    l_scr[...] = l_new
    acc_scr[...] = acc

    @pl.when(j == pl.num_programs(1) - 1)
    def _():
        o_ref[0] = acc_scr[...] / l_scr[...]


def _sample_attend(page_table, qb, sel, k_new, v_new, cache_k, cache_v, n_seq, n_pages):
    g = ATTEND_PAGES
    sel3 =sel.reshape(n_seq, 1, (n_pages + 1) * PAGE_ROWS)
    head = lambda: pl.BlockSpec((1, N_HEADS, HEAD_DIM), lambda b, j, pt: (b, 0, 0))
    page = lambda s: pl.BlockSpec((1, PAGE_SIZE, N_HEADS, HEAD_DIM),
                                  lambda b, j, pt: (pt[b * n_pages + j * g + s], 0, 0, 0))
    grid_spec = pltpu.PrefetchScalarGridSpec(
        num_scalar_prefetch=1,
        grid=(n_seq, n_pages // g),
        in_specs=[head(),
                  pl.BlockSpec((1, 1, g * PAGE_ROWS), lambda b, j, pt: (b, 0, j)),
                  pl.BlockSpec((1, 1, PAGE_ROWS), lambda b, j, pt: (b, 0, n_pages)),
                  head(), head()] + [page(s) for s in range(g)] * 2,
        out_specs=head(),
        scratch_shapes=[pltpu.VMEM((N_HEADS, 1), F32), pltpu.VMEM((N_HEADS, 1), F32),
                        pltpu.VMEM((N_HEADS, HEAD_DIM), F32)])
    to_heads = lambda z: z.reshape(n_seq, N_HEADS, HEAD_DIM)
    out = pl.pallas_call(
        _sample_attend_body,
        grid_spec=grid_spec,
        out_shape=jax.ShapeDtypeStruct((n_seq, N_HEADS, HEAD_DIM), F32),
        compiler_params=_params("parallel", "arbitrary"),
        name="sample_attend",
    )(page_table.reshape(-1), to_heads(qb), sel3, sel3, to_heads(k_new), to_heads(v_new),
      *([cache_k] * g), *([cache_v] * g))
    return out.reshape(n_seq, D_ATT)


def _layer_norm(x, g, b):
    mu = jnp.mean(x, axis=-1, keepdims=True)
    d = x - mu
    var = jnp.mean(d * d, axis=-1, keepdims=True)
    return d * lax.rsqrt(var + LN_EPS) * g + b


def _post_body(x_ref, orw_ref, oatt_ref, wo_ref, g_ref, b_ref, wr_ref, h_o, gate_o):
    mix = _dot(orw_ref[...].astype(BF16), wo_ref[0:D_RWKV, :]) + _dot(oatt_ref[...].astype(BF16), wo_ref[D_RWKV:, :])
    h = _layer_norm(ALPHA * x_ref[...] + mix, g_ref[...], b_ref[...])
    h_o[...] = h
    logit = _dot_hi(h, wr_ref[...])
    rows = logit.shape[0]
    lane = lax.broadcasted_iota(I32, (rows, LANES), 1).astype(F32)
    big = float(LANES)

    def first_max(mask):
        val = jnp.max(jnp.where(mask, logit, -jnp.inf), axis=1, keepdims=True)
        idx = jnp.min(jnp.where(jnp.logical_and(mask, logit == val), lane, big), axis=1, keepdims=True)
        return val, idx

    is_grp = jnp.logical_and(lane >= N_EXPERTS, lane < N_EXPERTS + N_GROUPS)
    g_max, g_lane = first_max(is_grp)
    p_grp = 1.0 / jnp.sum(jnp.where(is_grp, jnp.exp(logit - g_max), 0.0), axis=1, keepdims=True)
    in_grp = jnp.floor(lane * (1.0 / EXPERTS_PER_GROUP)) == (g_lane - N_EXPERTS)
    v1, i1 = first_max(in_grp)
    v2, i2 = first_max(jnp.logical_and(in_grp, lane != i1))
    e2 = jnp.exp(v2 - v1)
    g1 = p_grp / (1.0 + e2)
    g2 = p_grp * e2 / (1.0 + e2)
    gate_o[...] = jnp.where(lane == i1, g1, 0.0) + jnp.where(lane == i2, g2, 0.0)


def _post(x, o_rw, o_att, pw, tm):
    n = x.shape[0]
    row = lambda w: pl.BlockSpec((tm, w), lambda i: (i, 0))
    return pl.pallas_call(
        _post_body,
        grid=(n // tm,),
        in_specs=[row(D_MODEL), row(D_RWKV), row(D_ATT), _full((D_MODEL, D_MODEL)), _full((1, D_MODEL)),
                  _full((1, D_MODEL)), _full((D_MODEL, LANES))],
        out_specs=[row(D_MODEL), row(LANES)],
        out_shape=[jax.ShapeDtypeStruct((n, D_MODEL), F32), jax.ShapeDtypeStruct((n, LANES), F32)],
        compiler_params=_params("parallel"),
        name="post",
    )(x, o_rw, o_att, pw["wout"], pw["ln1g"], pw["ln1b"], pw["wrouter"])


def _moe_body(h_ref, gate_ref, w1_ref, w3_ref, w2_ref, g_ref, b_ref, y_ref, hb_scr, acc_scr):
    e = pl.program_id(1)

    @pl.when(e == 0)
    def _():
        hb_scr[...] = h_ref[...].astype(BF16)
        acc_scr[...] = jnp.zeros_like(acc_scr)

    hb = hb_scr[...]
    gates = gate_ref[0]
    for j in range(EXPERT_CHUNK):
        h1 = _dot(hb, w1_ref[j])
        h3 = _dot(hb, w3_ref[j])
        hidden = jax.nn.silu(h1) * h3 * gates[:, j:j + 1]
        acc_scr[...] += _dot(hidden.astype(BF16), w2_ref[j])

    @pl.when(e == pl.num_programs(1) - 1)
    def _():
        y_ref[...] = _layer_norm(ALPHA * h_ref[...] + acc_scr[...], g_ref[...], b_ref[...])


def _moe(h, gate, pw, tm):
    n = h.shape[0]
    n_ec = N_EXPERTS // EXPERT_CHUNK
    gate_c = gate[:, :N_EXPERTS].reshape(n, n_ec, EXPERT_CHUNK).transpose(1, 0, 2)
    return pl.pallas_call(
        _moe_body,
        grid=(n // tm, n_ec),
        in_specs=[pl.BlockSpec((tm, D_MODEL), lambda i, e: (i, 0)),
                  pl.BlockSpec((1, tm, EXPERT_CHUNK), lambda i, e: (e, i, 0)),
                  pl.BlockSpec((EXPERT_CHUNK, D_MODEL, D_EXPERT), lambda i, e: (e, 0, 0)),
                  pl.BlockSpec((EXPERT_CHUNK, D_MODEL, D_EXPERT), lambda i, e: (e, 0, 0)),
                  pl.BlockSpec((EXPERT_CHUNK, D_EXPERT, D_MODEL), lambda i, e: (e, 0, 0)),
                  _full((1, D_MODEL)), _full((1, D_MODEL))],
        out_specs=pl.BlockSpec((tm, D_MODEL), lambda i, e: (i, 0)),
        out_shape=jax.ShapeDtypeStruct((n, D_MODEL), F32),
        scratch_shapes=[pltpu.VMEM((tm, D_MODEL), BF16), pltpu.VMEM((tm, D_MODEL), F32)],
        compiler_params=_params("parallel", "arbitrary"),
        name="moe",
    )(h, gate_c, pw["w1"], pw["w3"], pw["w2"], pw["ln2g"], pw["ln2b"])


def _prepare_weights(w_in, rw_mu, rw_w0, rw_w_up, rw_a0, rw_a_up, rw_g_up, rw_k_k, rw_k_a, rw_r_k,
                     rw_ln_g, rw_ln_b, idx_ln_g, idx_ln_b, w_out, ln1_g, ln1_b,
                     moe_w_grp, moe_w_exp, moe_w1, moe_w3, moe_w2, ln2_g, ln2_b):
    row = lambda z: z.reshape(1, -1).astype(F32)
    c5 = RW_COLS + 4 * D_ATT
    pad_to = lambda z, w: jnp.pad(z, ((0, 0), (0, w - z.shape[1])))
    wtail = jnp.concatenate([pad_to(w_in[:, c5:c5 + IDX_DIM], LANES),
                             pad_to(w_in[:, c5 + IDX_DIM:], LANES)], axis=1)
    zeros_w = jnp.zeros((LORA_W, D_RWKV), F32)
    return {
        "wrw": w_in[:, :RW_COLS].astype(BF16),
        "watt": w_in[:, RW_COLS:c5].astype(BF16),
        "wtail": wtail.astype(BF16),
        "mu": row(rw_mu), "w0": row(rw_w0), "a0": row(rw_a0), "kk": row(rw_k_k), "ka": row(rw_k_a),
        "wup": jnp.concatenate([rw_w_up, zeros_w], 0).astype(BF16),
        "aup": jnp.concatenate([zeros_w, rw_a_up], 0).astype(BF16),
        "gup": rw_g_up.astype(BF16),
        "lng": pad_to(row(idx_ln_g), LANES), "lnb": pad_to(row(idx_ln_b), LANES),
        "rk": row(rw_r_k), "rwlng": row(rw_ln_g), "rwlnb": row(rw_ln_b),
        "wout": w_out.astype(BF16), "ln1g": row(ln1_g), "ln1b": row(ln1_b),
        "wrouter": pad_to(jnp.concatenate([moe_w_exp, moe_w_grp], 1).astype(F32), LANES),
        "w1": moe_w1.astype(BF16), "w3": moe_w3.astype(BF16), "w2": moe_w2.astype(BF16),
        "ln2g": row(ln2_g), "ln2b": row(ln2_b),
    }


def _row_tile(n, want):
    return min(n, want)


def _layer(x, x_shift, pos_tables, pw, attend, scan, *, tm_proj, tm_post, tm_moe):
    n = x.shape[0]
    outs = _proj(x, x_shift, pos_tables, pw, _row_tile(n, tm_proj))
    rw = outs[:7]
    qb, kf, vf, kb, vt, qib, kif, kib, wi = outs[7:]
    o_rw, s_fin = scan(rw)
    o_att = attend(qb, kf, vf, kb, vt, qib, kif, kib, wi)
    h, gate = _post(x, o_rw, o_att, pw, _row_tile(n, tm_post))
    y = _moe(h, gate, pw, _row_tile(n, tm_moe))
    return y, kf, vf, kif, s_fin


def kernel(x_prompt, x_sample, cache_k, cache_v, cache_kidx, state_wkv, state_shift, page_table, w_in, rw_mu, rw_w0, rw_w_up, rw_a0, rw_a_up, rw_g_up, rw_k_k, rw_k_a, rw_r_k, rw_ln_g, rw_ln_b, idx_ln_g, idx_ln_b, w_out, ln1_g, ln1_b, moe_w_grp, moe_w_exp, moe_w1, moe_w3, moe_w2, ln2_g, ln2_b):
    pw = _prepare_weights(w_in, rw_mu, rw_w0, rw_w_up, rw_a0, rw_a_up, rw_g_up, rw_k_k, rw_k_a, rw_r_k,
                          rw_ln_g, rw_ln_b, idx_ln_g, idx_ln_b, w_out, ln1_g, ln1_b,
                          moe_w_grp, moe_w_exp, moe_w1, moe_w3, moe_w2, ln2_g, ln2_b)
    bp, tp, _ = x_prompt.shape
    bs, ts, _ = x_sample.shape
    assert ts == 1
    n_pages = page_table.shape[1]
    past = n_pages * PAGE_SIZE
    assert n_pages % SCORE_PAGES == 0 and n_pages % ATTEND_PAGES == 0

    xp = x_prompt.reshape(bp * tp, D_MODEL)

    def attend_p(qb, kf, vf, kb, vt, qib, kif, kib, wi):
        return _dsa_prompt(qib, wi.T, qb, kib, kb, vt, bp, tp)

    def scan_p(rw):
        s0 = jnp.zeros((bp, N_HEADS, HEAD_DIM, HEAD_DIM), F32)
        return _scan_chunked(rw, s0, pw["rk"], pw["rwlng"], pw["rwlnb"], bp, tp)

    y_p, k_p, v_p, ki_p, wkv_p = _layer(xp, None, _rope_tables(jnp.arange(tp)),
                                        pw, attend_p, scan_p, tm_proj=256, tm_post=512, tm_moe=1024)

    xs = x_sample.reshape(bs, D_MODEL)

    def attend_s(qb, kf, vf, kb, vt, qib, kif, kib, wi):
        scores, score_new = _sample_scores(page_table, qib, wi, kif.astype(cache_kidx.dtype), cache_kidx, bs, n_pages)
        sel = _sample_select(scores.reshape(bs, past), score_new.reshape(bs, LANES), past)
        return _sample_attend(page_table, qb, sel, kf.astype(cache_k.dtype), vf.astype(cache_v.dtype),
                              cache_k, cache_v, bs, n_pages)

    def scan_s(rw):
        return _scan_step(rw, state_wkv.astype(F32), pw["rk"], pw["rwlng"], pw["rwlnb"], bs)

    y_s, k_s, v_s, ki_s, wkv_s = _layer(xs, state_shift.astype(x_sample.dtype),
                                        _rope_tables(jnp.full((bs,), past, I32)),
                                        pw, attend_s, scan_s, tm_proj=256, tm_post=512, tm_moe=1024)

    heads = lambda z, b, t: z.reshape(b, t, N_HEADS, HEAD_DIM)
    return (y_p.reshape(bp, tp, D_MODEL), y_s.reshape(bs, ts, D_MODEL),
            heads(k_p, bp, tp).astype(cache_k.dtype), heads(v_p, bp, tp).astype(cache_v.dtype),
            ki_p.reshape(bp, tp, IDX_DIM).astype(cache_kidx.dtype),
            wkv_p.astype(state_wkv.dtype), x_prompt[:, -1].astype(state_shift.dtype),
            heads(k_s, bs, ts).astype(cache_k.dtype), heads(v_s, bs, ts).astype(cache_v.dtype),
            ki_s.reshape(bs, ts, IDX_DIM).astype(cache_kidx.dtype),
            wkv_s.astype(state_wkv.dtype), x_sample[:, -1].astype(state_shift.dtype))
```

```python
import functools
import math

import jax
import jax.numpy as jnp
from jax import lax
from jax.experimental import pallas as pl
from jax.experimental.pallas import tpu as pltpu

F32 = jnp.float32
BF16 = jnp.bfloat16
I32 = jnp.int32

D_MODEL = 1024
HEAD_DIM = 64
D_RWKV = 512
D_ATT = 512
N_HEADS = 8
LORA_W = 64
LORA_A = 64
LORA_G = 128
N_IDX_HEADS = 8
IDX_DIM = 64
TOPK_MAX = 256
ROPE_THETA = 500000.0
PAGE_SIZE = 128
N_GROUPS = 4
EXPERTS_PER_GROUP = 8
N_EXPERTS = 32
D_EXPERT = 256
LN_EPS = 1e-5
GN_EPS = HEAD_DIM * 1e-5
DECAY_SCALE = math.exp(-0.5)
ALPHA = 2.0 ** 0.25
NEG = -1e30
RW_COLS = 3 * D_RWKV + LORA_W + LORA_A + LORA_G
WI_SCALE = N_IDX_HEADS ** -0.5 * IDX_DIM ** -0.5
QK_SCALE = HEAD_DIM ** -0.5 * math.log2(math.e)
INT_MIN = -(2 ** 31)
INT_MAX = 2 ** 31 - 1

LANES = 128
SUBLANES = 8
VMEM_LIMIT = 56 * 1024 * 1024
SCAN_CHUNK = 64
SOLVE_BLOCK = 16
EXPERT_CHUNK = 4
DSA_QUERIES = 512
DSA_KEYS = 256
ACC_ROWS = HEAD_DIM + 16
SCORE_PAGES = 8
ATTEND_PAGES = 8


def _dot(a, b):
    return jnp.dot(a, b, preferred_element_type=F32)


def _dot_nt(a, b):
    return lax.dot_general(a, b, (((1,), (1,)), ((), ())), preferred_element_type=F32)


def _dot_hi(a, b):
    return jnp.dot(a, b, preferred_element_type=F32, precision=lax.Precision.HIGHEST)


def _split(x):
    hi = x.astype(BF16)
    return hi, (x - hi.astype(F32)).astype(BF16)


def _dot3(a, b, nt=False):
    f = _dot_nt if nt else _dot
    ah, al = _split(a)
    bh, bl = _split(b)
    return f(ah, bh) + (f(ah, bl) + f(al, bh))


def _params(*sem):
    return pltpu.CompilerParams(dimension_semantics=sem, vmem_limit_bytes=VMEM_LIMIT)


def _full(shape):
    nd = len(shape)
    return pl.BlockSpec(shape, lambda *_: (0,) * nd, pipeline_mode=pl.Buffered(1))


def _rope(x, cc, s1, s2):
    outs = []
    for c in range(x.shape[1] // LANES):
        xc = x[:, c * LANES:(c + 1) * LANES]
        outs.append(xc * cc + pltpu.roll(xc, LANES - 8, 1) * s1 + pltpu.roll(xc, 8, 1) * s2)
    return outs[0] if len(outs) == 1 else jnp.concatenate(outs, axis=1)


def _proj_body(x_ref, *refs, tiles_per_seq):
    if tiles_per_seq:
        (wrw_ref, watt_ref, wtail_ref, mu_ref, w0_ref, wup_ref, a0_ref, aup_ref, gup_ref, kk_ref, ka_ref,
         lng_ref, lnb_ref, cc_ref, s1_ref, s2_ref, r_o, lw_o, k_o, v_o, kkr_o, a_o, g_o, qb_o, kf_o, vf_o, kb_o,
         vt_o, qib_o, kif_o, kib_o, wi_o, carry_scr) = refs
    else:
        (xs_ref, wrw_ref, watt_ref, wtail_ref, mu_ref, w0_ref, wup_ref, a0_ref, aup_ref, gup_ref, kk_ref, ka_ref,
         lng_ref, lnb_ref, cc_ref, s1_ref, s2_ref, r_o, lw_o, k_o, v_o, kkr_o, a_o, g_o, qb_o, kf_o, vf_o, kb_o,
         vt_o, qib_o, kif_o, kib_o, wi_o) = refs
        xsb = xs_ref[...].astype(BF16)
    xb = x_ref[...].astype(BF16)
    tm = x_ref.shape[0]
    if tiles_per_seq:
        first_row = lax.broadcasted_iota(I32, (tm, 1), 0) == 0

        @pl.when(pl.program_id(0) % tiles_per_seq == 0)
        def _():
            carry_scr[...] = jnp.zeros_like(carry_scr)

    def lerp(c0, c1):
        w = wrw_ref[:, c0:c1]
        cur = _dot(xb, w)
        if tiles_per_seq:
            prev = jnp.where(first_row, carry_scr[0:1, c0:c1], pltpu.roll(cur, 1, 0))
            carry_scr[0:1, c0:c1] = cur[tm - 1:tm, :]
        else:
            prev = _dot(xsb, w)
        return cur + mu_ref[:, c0:c1] * (prev - cur)

    r_o[...] = lerp(0, D_RWKV)
    k = lerp(D_RWKV, 2 * D_RWKV)
    v_o[...] = lerp(2 * D_RWKV, 3 * D_RWKV)
    lora = lerp(3 * D_RWKV, 3 * D_RWKV + LORA_W + LORA_A)
    dg = lerp(3 * D_RWKV + LORA_W + LORA_A, RW_COLS)
    lw_o[...] = -DECAY_SCALE * jax.nn.sigmoid(w0_ref[...] + _dot(jnp.tanh(lora).astype(BF16), wup_ref[...]))
    a = jax.nn.sigmoid(a0_ref[...] + _dot(lora.astype(BF16), aup_ref[...]))
    a_o[...] = a
    g_o[...] = _dot(jax.nn.sigmoid(dg).astype(BF16), gup_ref[...])
    kkr_o[...] = k * kk_ref[...]
    k_o[...] = k * (1.0 + (a - 1.0) * ka_ref[...])

    cc, s1, s2 = cc_ref[...], s1_ref[...], s2_ref[...]
    q = _rope(_dot(xb, watt_ref[:, 0:D_ATT]), cc, s1, s2)
    qb_o[...] = (q * QK_SCALE).astype(BF16)
    ka = _rope(_dot(xb, watt_ref[:, D_ATT:2 * D_ATT]), cc, s1, s2)
    kb_o[...] = ka.astype(BF16)
    if tiles_per_seq:
        kf_o[0] = ka.T
    else:
        kf_o[...] = ka
    va = _dot(xb, watt_ref[:, 2 * D_ATT:3 * D_ATT])
    vf_o[...] = va
    vt_o[...] = va.T.astype(BF16)
    qi = _rope(_dot(xb, watt_ref[:, 3 * D_ATT:4 * D_ATT]), cc, s1, s2)
    qib_o[...] = qi.astype(BF16)

    tail = _dot(xb, wtail_ref[...])
    t0 = tail[:, :LANES]
    in_ki = lax.broadcasted_iota(I32, (1, LANES), 1) < IDX_DIM
    mean = jnp.sum(jnp.where(in_ki, t0, 0.0), axis=1, keepdims=True) * (1.0 / IDX_DIM)
    d = jnp.where(in_ki, t0 - mean, 0.0)
    var = jnp.sum(d * d, axis=1, keepdims=True) * (1.0 / IDX_DIM)
    ki_wide = _rope(d * lax.rsqrt(var + LN_EPS) * lng_ref[...] + lnb_ref[...], cc, s1, s2)
    ki = ki_wide[:, :IDX_DIM]
    if tiles_per_seq:
        kif_o[0] = ki_wide.T[:IDX_DIM, :]
    else:
        kif_o[...] = ki
    kib_o[...] = ki.astype(BF16)
    wi_o[...] = tail[:, LANES:LANES + N_IDX_HEADS] * WI_SCALE


def _rope_tables(pos):
    half = HEAD_DIM // 8
    inv = ROPE_THETA ** (-jnp.arange(half, dtype=F32) / half)
    ang = pos.astype(F32)[:, None] * inv[None, :]
    cos, sin = jnp.cos(ang), jnp.sin(ang)
    n = pos.shape[0]
    one = jnp.ones((n, HEAD_DIM - 2 * half), F32)
    zero = jnp.zeros((n, HEAD_DIM - 2 * half), F32)
    zh = jnp.zeros((n, half), F32)
    cc = jnp.concatenate([cos, cos, one], 1)
    s1 = jnp.concatenate([-sin, zh, zero], 1)
    s2 = jnp.concatenate([zh, sin, zero], 1)
    return tuple(jnp.tile(t, (1, LANES // HEAD_DIM)) for t in (cc, s1, s2))


def _proj(x, xs, tables, pw, tm):
    n = x.shape[0]
    nt = tables[0].shape[0] // tm
    tiles_per_seq = nt if xs is None else 0
    row = lambda w: pl.BlockSpec((tm, w), lambda i: (i, 0))
    tab = pl.BlockSpec((tm, LANES), lambda i: (i % nt, 0))
    weights = (pw["wrw"], pw["watt"], pw["wtail"], pw["mu"], pw["w0"], pw["wup"], pw["a0"], pw["aup"],
               pw["gup"], pw["kk"], pw["ka"], pw["lng"], pw["lnb"])
    f = lambda w, dt=F32: jax.ShapeDtypeStruct((n, w), dt)
    if tiles_per_seq:
        cache_shape = lambda w: jax.ShapeDtypeStruct((n // (nt * tm), w, nt * tm), F32)
        cache_spec = lambda w: pl.BlockSpec((1, w, tm), lambda i: (i // nt, 0, i % nt))
    else:
        cache_shape, cache_spec = f, row
    out_shape = ([f(D_RWKV)] * 7 + [f(D_ATT, BF16), cache_shape(D_ATT), f(D_ATT), f(D_ATT, BF16),
                                    jax.ShapeDtypeStruct((D_ATT, n), BF16),
                                    f(D_ATT, BF16), cache_shape(IDX_DIM), f(IDX_DIM, BF16), f(N_IDX_HEADS)])
    out_specs = ([row(D_RWKV)] * 7 + [row(D_ATT), cache_spec(D_ATT), row(D_ATT), row(D_ATT),
                                      pl.BlockSpec((D_ATT, tm), lambda i: (0, i)),
                                      row(D_ATT), cache_spec(IDX_DIM), row(IDX_DIM), row(N_IDX_HEADS)])
    rows_in = (x,) if xs is None else (x, xs)
    return pl.pallas_call(
        functools.partial(_proj_body, tiles_per_seq=tiles_per_seq),
        grid=(n // tm,),
        in_specs=[row(D_MODEL)] * len(rows_in) + [_full(w.shape) for w in weights] + [tab] * 3,
        out_specs=out_specs,
        out_shape=out_shape,
        scratch_shapes=[pltpu.VMEM((SUBLANES, RW_COLS), F32)] if xs is None else [],
        compiler_params=_params("arbitrary" if xs is None else "parallel"),
        name="proj",
    )(*rows_in, *weights, *tables)


def _head_epilogue(o, r, k, v, g, rk, lng, lnb):
    mu = jnp.mean(o, axis=-1, keepdims=True)
    d = o - mu
    var = jnp.mean(d * d, axis=-1, keepdims=True)
    on = d * lax.rsqrt(var + GN_EPS) * lng + lnb
    bonus = jnp.sum(r * k * rk, axis=-1, keepdims=True) * v
    return (on + bonus) * g


def _scan_chunk_body(r_ref, lw_ref, k_ref, v_ref, kk_ref, a_ref, g_ref, rk_ref, lng_ref, lnb_ref, s0_ref,
                     o_ref, sf_ref, s_scr):
    c = pl.program_id(1)

    @pl.when(c == 0)
    def _():
        s_scr[...] = s0_ref[0]

    C = r_ref.shape[0]
    heads = range(N_HEADS)
    row = lax.broadcasted_iota(I32, (C, 2 * C), 0)
    col = lax.broadcasted_iota(I32, (C, 2 * C), 1) % C
    strict2 = col < row
    incl2 = col <= row
    same_blk = (col // SOLVE_BLOCK) == (row // SOLVE_BLOCK)
    left = lax.broadcasted_iota(I32, (C, 2 * C), 1) < C
    tri = incl2[:, :C].astype(BF16)

    lw_all = lw_ref[...]
    l_hi = lw_all.astype(BF16)
    l_r1 = lw_all - l_hi.astype(F32)
    l_mid = l_r1.astype(BF16)
    l_lo = (l_r1 - l_mid.astype(F32)).astype(BF16)
    cum_all = _dot(tri, l_hi) + (_dot(tri, l_mid) + _dot(tri, l_lo))

    sls = [slice(h * HEAD_DIM, (h + 1) * HEAD_DIM) for h in heads]
    r = [r_ref[:, sl] for sl in sls]
    k = [k_ref[:, sl] for sl in sls]
    v = [v_ref[:, sl] for sl in sls]
    lhs1, rhs1, bk_end, g_end = [], [], [], []
    for h in heads:
        sl = sls[h]
        lw, kkr, a = lw_all[:, sl], kk_ref[:, sl], a_ref[:, sl]
        cum = cum_all[:, sl]
        cum_end = cum[C - 1:C, :]
        kk = kkr * lax.rsqrt(jnp.maximum(jnp.sum(kkr * kkr, axis=-1, keepdims=True), 1e-24))
        beta = kk * a
        e_neg = jnp.exp(-cum)
        e_end = jnp.exp(cum_end - cum)
        lhs1.append(jnp.concatenate([-kk * jnp.exp(cum - lw), r[h] * jnp.exp(cum)], axis=0))
        rhs1.append(jnp.concatenate([beta * e_neg, k[h] * e_neg, s_scr[h]], axis=0))
        bk_end.append(jnp.concatenate([beta * e_end, k[h] * e_end], axis=0))
        g_end.append(jnp.exp(cum_end))

    r1 = [_dot3(lhs1[h], rhs1[h], nt=True) for h in heads]
    vv = [jnp.concatenate([v[h], v[h]], axis=0) for h in heads]
    y = [r1[h][:C, 2 * C:] + _dot3(jnp.where(jnp.logical_and(strict2, ~left), r1[h][:C, :2 * C], 0.0), vv[h])
         for h in heads]
    n_pair = [r1[h][:C, :2 * C] for h in heads]
    p = [jnp.where(jnp.logical_and(strict2, same_blk), n_pair[h], 0.0)[:, :C] for h in heads]
    x = [jnp.where(left, jnp.where(jnp.logical_and(strict2, ~same_blk), n_pair[h], 0.0),
                   jnp.concatenate([y[h], y[h]], axis=1)) for h in heads]
    n_steps = SOLVE_BLOCK.bit_length() - 1
    for i in range(n_steps):
        if i + 1 < n_steps:
            rr = [_dot3(p[h], jnp.concatenate([x[h], p[h]], axis=1)) for h in heads]
            x = [x[h] + rr[h][:, :2 * C] for h in heads]
            p = [rr[h][:, 2 * C:] for h in heads]
        else:
            x = [x[h] + _dot3(p[h], x[h]) for h in heads]
    rr = [_dot3(x[h][:, :C], x[h]) for h in heads]
    x = [jnp.where(left, rr[h], x[h] + rr[h]) for h in heads]
    rr = [_dot3(x[h][:, :C], x[h]) for h in heads]
    u = [(x[h] + rr[h])[:, C:] for h in heads]

    uv = [jnp.concatenate([u[h], v[h]], axis=0) for h in heads]
    o = [r1[h][C:, 2 * C:] + _dot3(jnp.where(incl2, r1[h][C:, :2 * C], 0.0), uv[h]) for h in heads]
    for h in heads:
        s_new = rhs1[h][2 * C:, :] * g_end[h] + _dot3(uv[h].T, bk_end[h])
        s_scr[h] = s_new
        sl = sls[h]
        o_ref[:, sl] = _head_epilogue(o[h], r[h], k[h], v[h], g_ref[:, sl], rk_ref[:, sl], lng_ref[:, sl],
                                      lnb_ref[:, sl])

    @pl.when(c == pl.num_programs(1) - 1)
    def _():
        sf_ref[0] = s_scr[...]


def _scan_chunked(rw, s0, rk, lng, lnb, n_seq, t):
    nc = t // SCAN_CHUNK
    row = pl.BlockSpec((SCAN_CHUNK, D_RWKV), lambda b, c: (b * nc + c, 0))
    st = pl.BlockSpec((1, N_HEADS, HEAD_DIM, HEAD_DIM), lambda b, c: (b, 0, 0, 0))
    par = _full((1, D_RWKV))
    return pl.pallas_call(
        _scan_chunk_body,
        grid=(n_seq, nc),
        in_specs=[row] * 7 + [par] * 3 + [st],
        out_specs=[row, st],
        out_shape=[jax.ShapeDtypeStruct((n_seq * t, D_RWKV), F32),
                   jax.ShapeDtypeStruct((n_seq, N_HEADS, HEAD_DIM, HEAD_DIM), F32)],
        scratch_shapes=[pltpu.VMEM((N_HEADS, HEAD_DIM, HEAD_DIM), F32)],
        compiler_params=_params("parallel", "arbitrary"),
        name="scan_chunked",
    )(*rw, rk, lng, lnb, s0)


def _scan_step_body(r_ref, lw_ref, k_ref, v_ref, kk_ref, a_ref, g_ref, rk_ref, lng_ref, lnb_ref, s0_ref,
                    o_ref, sf_ref):
    eye = (lax.broadcasted_iota(I32, (HEAD_DIM, HEAD_DIM), 0)
           == lax.broadcasted_iota(I32, (HEAD_DIM, HEAD_DIM), 1)).astype(F32)
    outs = []
    for h in range(N_HEADS):
        sl = slice(h * HEAD_DIM, (h + 1) * HEAD_DIM)
        r, lw, k, v = r_ref[0, :, sl], lw_ref[0, :, sl], k_ref[0, :, sl], v_ref[0, :, sl]
        kkr, a = kk_ref[0, :, sl], a_ref[0, :, sl]
        kk = kkr * lax.rsqrt(jnp.maximum(jnp.sum(kkr * kkr, axis=-1, keepdims=True), 1e-24))
        s = s0_ref[0, h]
        sa = jnp.sum(s * kk, axis=1, keepdims=True)
        v_col = jnp.sum(eye * v, axis=1, keepdims=True)
        s = s * jnp.exp(lw) - sa * (kk * a) + v_col * k
        sf_ref[0, h] = s
        o_col = jnp.sum(s * r, axis=1, keepdims=True)
        o = jnp.sum(eye * o_col, axis=0, keepdims=True)
        outs.append(_head_epilogue(o, r, k, v, g_ref[0, :, sl], rk_ref[:, sl], lng_ref[:, sl], lnb_ref[:, sl]))
    o_ref[0] = jnp.concatenate(outs, axis=1)


def _scan_step(rw, s0, rk, lng, lnb, n_seq):
    row = pl.BlockSpec((1, 1, D_RWKV), lambda b: (b, 0, 0))
    st = pl.BlockSpec((1, N_HEADS, HEAD_DIM, HEAD_DIM), lambda b: (b, 0, 0, 0))
    par = _full((1, D_RWKV))
    rw3 = [z.reshape(n_seq, 1, D_RWKV) for z in rw]
    o, sf = pl.pallas_call(
        _scan_step_body,
        grid=(n_seq,),
        in_specs=[row] * 7 + [par] * 3 + [st],
        out_specs=[row, st],
        out_shape=[jax.ShapeDtypeStruct((n_seq, 1, D_RWKV), F32),
                   jax.ShapeDtypeStruct((n_seq, N_HEADS, HEAD_DIM, HEAD_DIM), F32)],
        compiler_params=_params("parallel"),
        name="scan_step",
    )(*rw3, rk, lng, lnb, s0)
    return o.reshape(n_seq, D_RWKV), sf


def _order_key(score):
    bits = pltpu.bitcast(score, I32)
    return bits ^ ((bits >> 31) & INT_MAX)


def _select_threshold(count_ge, count_tie_le, n_valid, ktop, idx_bits):
    shape = n_valid.shape
    lo = jnp.full(shape, INT_MIN + 1, I32)
    hi = jnp.where(n_valid <= ktop, lo + 1, INT_MAX)

    def cond(st):
        it, lo, hi, _ = st
        return jnp.logical_and(it < 33, jnp.max(jnp.where(lo + 1 != hi, 1.0, 0.0)) > 0.0)

    def body(st):
        it, lo, hi, c_lo = st
        mid = (lo & hi) + ((lo ^ hi) >> 1)
        cnt = count_ge(mid)
        live = lo + 1 != hi
        up = jnp.logical_and(live, cnt >= ktop)
        down = jnp.logical_and(live, cnt < ktop)
        hi2 = jnp.where(jnp.logical_and(up, cnt == ktop), mid + 1, jnp.where(down, mid, hi))
        return it + 1, jnp.where(up, mid, lo), hi2, jnp.where(up, cnt, c_lo)

    _, thr, _, c_thr = lax.while_loop(cond, body, (jnp.int32(0), lo, hi, n_valid.astype(F32)))
    excess = jnp.logical_and(c_thr > ktop, n_valid > ktop)
    has_tie = jnp.max(jnp.where(excess, 1.0, 0.0)) > 0.0
    all_idx = jnp.full(shape, (1 << idx_bits) - 1, I32)

    def tie_search():
        need = ktop - count_ge(thr + 1)

        def tbody(_, st):
            lo_j, hi_j = st
            mid = (lo_j + hi_j) >> 1
            ok = count_tie_le(thr, mid) >= need
            return jnp.where(ok, lo_j, mid + 1), jnp.where(ok, mid, hi_j)

        lo_j, _ = lax.fori_loop(0, idx_bits, tbody, (jnp.zeros(shape, I32), all_idx))
        return jnp.where(excess, lo_j, all_idx)

    tie_end = lax.cond(has_tie, tie_search, lambda: all_idx)
    return thr, tie_end, has_tie


def _dsa_prompt_body(qi_ref, wit_ref, q_ref, ki_ref, k_ref, vt_ref, o_ref, key_scr, m_scr, bias_scr, s_scr, acc_scr,
                     *, ktop, tq, tk, idx_bits):
    i = pl.program_id(1)
    n_kt = (i + 1) * (tq // tk)
    qpos = i * tq + lax.broadcasted_iota(I32, (1, tq), 1)
    key_off = lax.broadcasted_iota(I32, (tk, 1), 0)
    tile = lambda kt: pl.ds(pl.multiple_of(kt * tk, tk), tk)
    qi = qi_ref[...]
    qi_h = [qi[:, h * IDX_DIM:(h + 1) * IDX_DIM] for h in range(N_IDX_HEADS)]
    wit = wit_ref[...]

    def score_tile(kt, carry):
        kis = ki_ref[tile(kt), :]
        acc = jnp.zeros((tk, tq), F32)
        for h in range(N_IDX_HEADS):
            acc = acc + jnp.maximum(_dot_nt(kis, qi_h[h]), 0.0) * wit[h:h + 1, :]
        key_scr[tile(kt), :] = jnp.where(kt * tk + key_off <= qpos, _order_key(acc), INT_MIN)
        return carry

    lax.fori_loop(0, n_kt, score_tile, 0)

    def count(pred):
        def body(kt, acc):
            hit = jnp.where(pred(key_scr[tile(kt), :], kt * tk + key_off), 1.0, 0.0)
            return acc + jnp.sum(hit.reshape(tk // SUBLANES, SUBLANES, tq), axis=0)
        acc = lax.fori_loop(0, n_kt, body, jnp.zeros((SUBLANES, tq), F32))
        return jnp.sum(acc, axis=0, keepdims=True)

    thr, tie_end, has_tie = _select_threshold(
        lambda t: count(lambda keys, pos: keys >= t),
        lambda t, j: count(lambda keys, pos: jnp.logical_and(keys == t, pos <= j)),
        qpos + 1, ktop, idx_bits)

    @pl.when(has_tie)
    def _():
        def demote(kt, carry):
            keys = key_scr[tile(kt), :]
            late = jnp.logical_and(keys == thr, kt * tk + key_off > tie_end)
            key_scr[tile(kt), :] = jnp.where(late, thr - 1, keys)
            return carry
        lax.fori_loop(0, n_kt, demote, 0)

    q = q_ref[...]
    lane = lax.broadcasted_iota(I32, (1, LANES), 1)
    q_pad = []
    for h in range(N_HEADS):
        own = (lane < HEAD_DIM) if h % 2 == 0 else (lane >= HEAD_DIM)
        q_pad.append(jnp.where(own, q[:, (h // 2) * LANES:(h // 2 + 1) * LANES].astype(F32), 0.0).astype(BF16))
    m_scr[...] = jnp.full(m_scr.shape, NEG, F32)
    acc_scr[...] = jnp.zeros(acc_scr.shape, F32)
    ones = jnp.ones((ACC_ROWS - HEAD_DIM, tk), BF16)

    def attend(kt, carry):
        bias_scr[...] = jnp.where(key_scr[tile(kt), :] >= thr, 0.0, NEG)
        tile_max = []
        for h in range(N_HEADS):
            s = _dot_nt(k_ref[tile(kt), (h // 2) * LANES:(h // 2 + 1) * LANES], q_pad[h]) + bias_scr[...]
            s_scr[h] = s
            tile_max.append(jnp.max(s, axis=0, keepdims=True))
        m_old = m_scr[...]
        m_new = jnp.maximum(m_old, jnp.concatenate(tile_max, axis=0))
        alpha = jnp.exp2(m_old - m_new)
        m_scr[...] = m_new
        for h in range(N_HEADS):
            p = jnp.exp2(s_scr[h] - m_new[h:h + 1, :]).astype(BF16)
            v_ext = jnp.concatenate([vt_ref[h * HEAD_DIM:(h + 1) * HEAD_DIM, tile(kt)], ones], axis=0)
            rows = slice(h * ACC_ROWS, (h + 1) * ACC_ROWS)
            acc_scr[rows, :] = alpha[h:h + 1, :] * acc_scr[rows, :] + _dot(v_ext, p)
        return carry

    lax.fori_loop(0, n_kt, attend, 0)
    out_t = jnp.concatenate(
        [acc_scr[h * ACC_ROWS:h * ACC_ROWS + HEAD_DIM, :] / acc_scr[h * ACC_ROWS + HEAD_DIM:h * ACC_ROWS + HEAD_DIM + 1, :]
         for h in range(N_HEADS)], axis=0)
    o_ref[...] = out_t.T


def _dsa_prompt(qib, wit, qb, kib, kb, vt, n_seq, t):
    tq = min(t, DSA_QUERIES)
    tk = min(t, DSA_KEYS)
    nq = t // tq
    ktop = min(TOPK_MAX, t // 4)
    qrow = lambda w: pl.BlockSpec((tq, w), lambda b, i: (b * nq + i, 0))
    seq = lambda w: pl.BlockSpec((t, w), lambda b, i: (b, 0), pipeline_mode=pl.Buffered(1))
    body = functools.partial(_dsa_prompt_body, ktop=ktop, tq=tq, tk=tk, idx_bits=max(1, (t - 1).bit_length()))
    return pl.pallas_call(
        body,
        grid=(n_seq, nq),
        in_specs=[qrow(D_ATT), pl.BlockSpec((N_IDX_HEADS, tq), lambda b, i: (0, b * nq + i)), qrow(D_ATT),
                  seq(IDX_DIM), seq(D_ATT),
                  pl.BlockSpec((D_ATT, t), lambda b, i: (0, b), pipeline_mode=pl.Buffered(1))],
        out_specs=qrow(D_ATT),
        out_shape=jax.ShapeDtypeStruct((n_seq * t, D_ATT), F32),
        scratch_shapes=[pltpu.VMEM((t, tq), I32), pltpu.VMEM((N_HEADS, tq), F32), pltpu.VMEM((tk, tq), F32),
                        pltpu.VMEM((N_HEADS, tk, tq), F32), pltpu.VMEM((N_HEADS * ACC_ROWS, tq), F32)],
        compiler_params=_params("parallel", "arbitrary"),
        name="dsa_prompt",
    )(qib, wit, qb, kib, kb, vt)


def _sample_score_body(pt_ref, qi_ref, wi_ref, kin_ref, *rest):
    page_refs, (sc_ref, scn_ref) = rest[:SCORE_PAGES], rest[SCORE_PAGES:]
    qi = qi_ref[0]
    wi_col = wi_ref[0]
    pages = jnp.concatenate([p[0] for p in page_refs], axis=1).astype(BF16)
    rel = jnp.maximum(_dot(qi, pages), 0.0) * wi_col
    sc_ref[0] = 0.0 + jnp.sum(rel, axis=0, keepdims=True)

    @pl.when(pl.program_id(1) == 0)
    def _():
        kn = kin_ref[0].astype(BF16).astype(F32)
        rel_n = jnp.maximum(jnp.sum(qi.astype(F32) * kn, axis=1, keepdims=True), 0.0) * wi_col
        scn_ref[0] = jnp.broadcast_to(0.0 + jnp.sum(rel_n, axis=0, keepdims=True), (1, LANES))


def _sample_scores(page_table, qib, wi, ki_new, kidx_t, n_seq, n_pages):
    g = SCORE_PAGES
    page = lambda s: pl.BlockSpec((1, IDX_DIM, PAGE_SIZE), lambda b, j, pt: (pt[b * n_pages + j * g + s], 0, 0))
    grid_spec = pltpu.PrefetchScalarGridSpec(
        num_scalar_prefetch=1,
        grid=(n_seq, n_pages // g),
        in_specs=[pl.BlockSpec((1, N_IDX_HEADS, IDX_DIM), lambda b, j, pt: (b, 0, 0)),
                  pl.BlockSpec((1, N_IDX_HEADS, 1), lambda b, j, pt: (b, 0, 0)),
                  pl.BlockSpec((1, 1, IDX_DIM), lambda b, j, pt: (b, 0, 0))] + [page(s) for s in range(g)],
        out_specs=[pl.BlockSpec((1, 1, g * PAGE_SIZE), lambda b, j, pt: (b, 0, j)),
                   pl.BlockSpec((1, 1, LANES), lambda b, j, pt: (b, 0, 0))])
    return pl.pallas_call(
        _sample_score_body,
        grid_spec=grid_spec,
        out_shape=[jax.ShapeDtypeStruct((n_seq, 1, n_pages * PAGE_SIZE), F32),
                   jax.ShapeDtypeStruct((n_seq, 1, LANES), F32)],
        compiler_params=_params("parallel", "arbitrary"),
        name="sample_scores",
    )(page_table.reshape(-1), qib.reshape(n_seq, N_IDX_HEADS, IDX_DIM), wi.reshape(n_seq, N_IDX_HEADS, 1),
      ki_new.reshape(n_seq, 1, IDX_DIM), *([kidx_t] * g))


def _sample_select_body(sc_ref, scn_ref, sel_ref, key_scr, *, ktop, past, idx_bits):
    rows = sc_ref.shape[0]
    n_t = past // LANES + 1
    lane_pos = lax.broadcasted_iota(I32, (1, LANES), 1)
    tile = lambda kt: pl.ds(pl.multiple_of(kt * LANES, LANES), LANES)
    key_scr[:, :past] = _order_key(sc_ref[...])
    key_scr[:, past:] = jnp.where(lane_pos == 0, _order_key(scn_ref[...]), INT_MIN)

    def count(pred):
        def body(kt, acc):
            return acc + jnp.where(pred(key_scr[:, tile(kt)], kt * LANES + lane_pos), 1.0, 0.0)
        acc = lax.fori_loop(0, n_t, body, jnp.zeros((rows, LANES), F32))
        return jnp.sum(acc, axis=1, keepdims=True)

    thr, tie_end, _ = _select_threshold(
        lambda t: count(lambda keys, pos: keys >= t),
        lambda t, j: count(lambda keys, pos: jnp.logical_and(keys == t, pos <= j)),
        jnp.full((rows, 1), past + 1, I32), ktop, idx_bits)

    def write(kt, carry):
        keys = key_scr[:, tile(kt)]
        sel = jnp.logical_or(keys > thr, jnp.logical_and(keys >= thr, kt * LANES + lane_pos <= tie_end))
        sel_ref[:, tile(kt)] = jnp.where(sel, 1.0, 0.0)
        return carry

    lax.fori_loop(0, n_t, write, 0)


def _sample_select(scores, score_new, past):
    rows = scores.shape[0]
    width = past + LANES
    ktop = min(TOPK_MAX, (past + 1) // 4)
    body = functools.partial(_sample_select_body, ktop=ktop, past=past, idx_bits=width.bit_length())
    rb = min(rows, SUBLANES)
    blk = lambda w: pl.BlockSpec((rb, w), lambda i: (i, 0))
    return pl.pallas_call(
        body,
        grid=(rows // rb,),
        in_specs=[blk(past), blk(LANES)],
        out_specs=blk(width),
        out_shape=jax.ShapeDtypeStruct((rows, width), F32),
        scratch_shapes=[pltpu.VMEM((rb, width), I32)],
        compiler_params=_params("parallel"),
        name="sample_select",
    )(scores, score_new)


def _sample_attend_body(pt_ref, q_ref, sel_ref, seln_ref, kn_ref, vn_ref, *rest):
    g = ATTEND_PAGES
    kp_refs, vp_refs, (o_ref, m_scr, l_scr, acc_scr) = rest[:g], rest[g:2 * g], rest[2 * g:]
    head_tokens = lambda refs, h: jnp.concatenate([r[0, h] for r in refs], axis=1).astype(BF16)
    j = pl.program_id(1)
    q = q_ref[0]
    row = lax.broadcasted_iota(I32, (N_HEADS, 1), 0)

    @pl.when(j == 0)
    def _():
        kn = kn_ref[0].astype(BF16).astype(F32)
        logit = jnp.sum(q.astype(F32) * kn, axis=1, keepdims=True)
        on = seln_ref[0][:, 0:1] > 0.0
        m_scr[...] = jnp.where(on, logit, NEG)
        l_scr[...] = jnp.where(on, jnp.ones_like(logit), 0.0)
        acc_scr[...] = jnp.where(on, vn_ref[0].astype(BF16).astype(F32), 0.0)

    on = sel_ref[0] > 0.0
    s = jnp.zeros((N_HEADS, g * PAGE_SIZE), F32)
    for h in range(N_HEADS):
        s = jnp.where(row == h, _dot(q, head_tokens(kp_refs, h)), s)
    s = jnp.where(on, s, NEG)
    m_old = m_scr[...]
    m_new = jnp.maximum(m_old, jnp.max(s, axis=1, keepdims=True))
    alpha = jnp.exp2(m_old - m_new)
    p = jnp.where(on, jnp.exp2(s - m_new), 0.0)
    m_scr[...] = m_new
    l_scr[...] = alpha * l_scr[...] + jnp.sum(p, axis=1, keepdims=True)
    pb = p.astype(BF16)
    acc = alpha * acc_scr[...]
    for h in range(N_HEADS):
        acc = acc + jnp.where(row == h, _dot_nt(pb, head_tokens(vp_refs, h)), 0.0)
    acc_scr[...] = acc

    @pl.when(j == pl.num_programs(1) - 1)
    def _():
        o_ref[0] = acc_scr[...] / l_scr[...]


def _sample_attend(page_table, qb, sel, k_new, v_new, k_t, v_t, n_seq, n_pages):
    g = ATTEND_PAGES
    sel3 = sel.reshape(n_seq, 1, (n_pages + 1) * PAGE_SIZE)
    head = lambda: pl.BlockSpec((1, N_HEADS, HEAD_DIM), lambda b, j, pt: (b, 0, 0))
    page = lambda s: pl.BlockSpec((1, N_HEADS, HEAD_DIM, PAGE_SIZE),
                                  lambda b, j, pt: (pt[b * n_pages + j * g + s], 0, 0, 0))
    grid_spec = pltpu.PrefetchScalarGridSpec(
        num_scalar_prefetch=1,
        grid=(n_seq, n_pages // g),
        in_specs=[head(),
                  pl.BlockSpec((1, 1, g * PAGE_SIZE), lambda b, j, pt: (b, 0, j)),
                  pl.BlockSpec((1, 1, PAGE_SIZE), lambda b, j, pt: (b, 0, n_pages)),
                  head(), head()] + [page(s) for s in range(g)] * 2,
        out_specs=head(),
        scratch_shapes=[pltpu.VMEM((N_HEADS, 1), F32), pltpu.VMEM((N_HEADS, 1), F32),
                        pltpu.VMEM((N_HEADS, HEAD_DIM), F32)])
    to_heads = lambda z: z.reshape(n_seq, N_HEADS, HEAD_DIM)
    out = pl.pallas_call(
        _sample_attend_body,
        grid_spec=grid_spec,
        out_shape=jax.ShapeDtypeStruct((n_seq, N_HEADS, HEAD_DIM), F32),
        compiler_params=_params("parallel", "arbitrary"),
        name="sample_attend",
    )(page_table.reshape(-1), to_heads(qb), sel3, sel3, to_heads(k_new), to_heads(v_new),
      *([k_t] * g), *([v_t] * g))
    return out.reshape(n_seq, D_ATT)


def _layer_norm(x, g, b):
    mu = jnp.mean(x, axis=-1, keepdims=True)
    d = x - mu
    var = jnp.mean(d * d, axis=-1, keepdims=True)
    return d * lax.rsqrt(var + LN_EPS) * g + b


def _post_body(x_ref, orw_ref, oatt_ref, wo_ref, g_ref, b_ref, wr_ref, h_o, gate_o):
    mix = _dot(orw_ref[...].astype(BF16), wo_ref[0:D_RWKV, :]) + _dot(oatt_ref[...].astype(BF16), wo_ref[D_RWKV:, :])
    h = _layer_norm(ALPHA * x_ref[...] + mix, g_ref[...], b_ref[...])
    h_o[...] = h
    logit = _dot_hi(h, wr_ref[...])
    rows = logit.shape[0]
    lane = lax.broadcasted_iota(I32, (rows, LANES), 1).astype(F32)
    big = float(LANES)

    def first_max(mask):
        val = jnp.max(jnp.where(mask, logit, -jnp.inf), axis=1, keepdims=True)
        idx = jnp.min(jnp.where(jnp.logical_and(mask, logit == val), lane, big), axis=1, keepdims=True)
        return val, idx

    is_grp = jnp.logical_and(lane >= N_EXPERTS, lane < N_EXPERTS + N_GROUPS)
    g_max, g_lane = first_max(is_grp)
    p_grp = 1.0 / jnp.sum(jnp.where(is_grp, jnp.exp(logit - g_max), 0.0), axis=1, keepdims=True)
    in_grp = jnp.floor(lane * (1.0 / EXPERTS_PER_GROUP)) == (g_lane - N_EXPERTS)
    v1, i1 = first_max(in_grp)
    v2, i2 = first_max(jnp.logical_and(in_grp, lane != i1))
    e2 = jnp.exp(v2 - v1)
    g1 = p_grp / (1.0 + e2)
    g2 = p_grp * e2 / (1.0 + e2)
    gate_o[...] = jnp.where(lane == i1, g1, 0.0) + jnp.where(lane == i2, g2, 0.0)


def _post(x, o_rw, o_att, pw, tm):
    n = x.shape[0]
    row = lambda w: pl.BlockSpec((tm, w), lambda i: (i, 0))
    return pl.pallas_call(
        _post_body,
        grid=(n // tm,),
        in_specs=[row(D_MODEL), row(D_RWKV), row(D_ATT), _full((D_MODEL, D_MODEL)), _full((1, D_MODEL)),
                  _full((1, D_MODEL)), _full((D_MODEL, LANES))],
        out_specs=[row(D_MODEL), row(LANES)],
        out_shape=[jax.ShapeDtypeStruct((n, D_MODEL), F32), jax.ShapeDtypeStruct((n, LANES), F32)],
        compiler_params=_params("parallel"),
        name="post",
    )(x, o_rw, o_att, pw["wout"], pw["ln1g"], pw["ln1b"], pw["wrouter"])


def _moe_body(h_ref, gate_ref, w1_ref, w3_ref, w2_ref, g_ref, b_ref, y_ref, hb_scr, acc_scr):
    e = pl.program_id(1)

    @pl.when(e == 0)
    def _():
        hb_scr[...] = h_ref[...].astype(BF16)
        acc_scr[...] = jnp.zeros_like(acc_scr)

    hb = hb_scr[...]
    gates = gate_ref[0]
    for j in range(EXPERT_CHUNK):
        h1 = _dot(hb, w1_ref[j])
        h3 = _dot(hb, w3_ref[j])
        hidden = jax.nn.silu(h1) * h3 * gates[:, j:j + 1]
        acc_scr[...] += _dot(hidden.astype(BF16), w2_ref[j])

    @pl.when(e == pl.num_programs(1) - 1)
    def _():
        y_ref[...] = _layer_norm(ALPHA * h_ref[...] + acc_scr[...], g_ref[...], b_ref[...])


def _moe(h, gate, pw, tm):
    n = h.shape[0]
    n_ec = N_EXPERTS // EXPERT_CHUNK
    gate_c = gate[:, :N_EXPERTS].reshape(n, n_ec, EXPERT_CHUNK).transpose(1, 0, 2)
    return pl.pallas_call(
        _moe_body,
        grid=(n // tm, n_ec),
        in_specs=[pl.BlockSpec((tm, D_MODEL), lambda i, e: (i, 0)),
                  pl.BlockSpec((1, tm, EXPERT_CHUNK), lambda i, e: (e, i, 0)),
                  pl.BlockSpec((EXPERT_CHUNK, D_MODEL, D_EXPERT), lambda i, e: (e, 0, 0)),
                  pl.BlockSpec((EXPERT_CHUNK, D_MODEL, D_EXPERT), lambda i, e: (e, 0, 0)),
                  pl.BlockSpec((EXPERT_CHUNK, D_EXPERT, D_MODEL), lambda i, e: (e, 0, 0)),
                  _full((1, D_MODEL)), _full((1, D_MODEL))],
        out_specs=pl.BlockSpec((tm, D_MODEL), lambda i, e: (i, 0)),
        out_shape=jax.ShapeDtypeStruct((n, D_MODEL), F32),
        scratch_shapes=[pltpu.VMEM((tm, D_MODEL), BF16), pltpu.VMEM((tm, D_MODEL), F32)],
        compiler_params=_params("parallel", "arbitrary"),
        name="moe",
    )(h, gate_c, pw["w1"], pw["w3"], pw["w2"], pw["ln2g"], pw["ln2b"])


def _prepare_weights(w_in, rw_mu, rw_w0, rw_w_up, rw_a0, rw_a_up, rw_g_up, rw_k_k, rw_k_a, rw_r_k,
                     rw_ln_g, rw_ln_b, idx_ln_g, idx_ln_b, w_out, ln1_g, ln1_b,
                     moe_w_grp, moe_w_exp, moe_w1, moe_w3, moe_w2, ln2_g, ln2_b):
    row = lambda z: z.reshape(1, -1).astype(F32)
    c5 = RW_COLS + 4 * D_ATT
    pad_to = lambda z, w: jnp.pad(z, ((0, 0), (0, w - z.shape[1])))
    wtail = jnp.concatenate([pad_to(w_in[:, c5:c5 + IDX_DIM], LANES),
                             pad_to(w_in[:, c5 + IDX_DIM:], LANES)], axis=1)
    zeros_w = jnp.zeros((LORA_W, D_RWKV), F32)
    return {
        "wrw": w_in[:, :RW_COLS].astype(BF16),
        "watt": w_in[:, RW_COLS:c5].astype(BF16),
        "wtail": wtail.astype(BF16),
        "mu": row(rw_mu), "w0": row(rw_w0), "a0": row(rw_a0), "kk": row(rw_k_k), "ka": row(rw_k_a),
        "wup": jnp.concatenate([rw_w_up, zeros_w], 0).astype(BF16),
        "aup": jnp.concatenate([zeros_w, rw_a_up], 0).astype(BF16),
        "gup": rw_g_up.astype(BF16),
        "lng": pad_to(row(idx_ln_g), LANES), "lnb": pad_to(row(idx_ln_b), LANES),
        "rk": row(rw_r_k), "rwlng": row(rw_ln_g), "rwlnb": row(rw_ln_b),
        "wout": w_out.astype(BF16), "ln1g": row(ln1_g), "ln1b": row(ln1_b),
        "wrouter": pad_to(jnp.concatenate([moe_w_exp, moe_w_grp], 1).astype(F32), LANES),
        "w1": moe_w1.astype(BF16), "w3": moe_w3.astype(BF16), "w2": moe_w2.astype(BF16),
        "ln2g": row(ln2_g), "ln2b": row(ln2_b),
    }


def _row_tile(n, want):
    return min(n, want)


def _layer(x, x_shift, pos_tables, pw, attend, scan, *, tm_proj, tm_post, tm_moe):
    n = x.shape[0]
    outs = _proj(x, x_shift, pos_tables, pw, _row_tile(n, tm_proj))
    rw = outs[:7]
    qb, kf, vf, kb, vt, qib, kif, kib, wi = outs[7:]
    o_rw, s_fin = scan(rw)
    o_att = attend(qb, kf, vf, kb, vt, qib, kif, kib, wi)
    h, gate = _post(x, o_rw, o_att, pw, _row_tile(n, tm_post))
    y = _moe(h, gate, pw, _row_tile(n, tm_moe))
    return y, kf, vf, kif, s_fin


def kernel(x_prompt, x_sample, cache_k, cache_v, cache_kidx, state_wkv, state_shift, page_table, w_in, rw_mu, rw_w0, rw_w_up, rw_a0, rw_a_up, rw_g_up, rw_k_k, rw_k_a, rw_r_k, rw_ln_g, rw_ln_b, idx_ln_g, idx_ln_b, w_out, ln1_g, ln1_b, moe_w_grp, moe_w_exp, moe_w1, moe_w3, moe_w2, ln2_g, ln2_b):
    pw = _prepare_weights(w_in, rw_mu, rw_w0, rw_w_up, rw_a0, rw_a_up, rw_g_up, rw_k_k, rw_k_a, rw_r_k,
                          rw_ln_g, rw_ln_b, idx_ln_g, idx_ln_b, w_out, ln1_g, ln1_b,
                          moe_w_grp, moe_w_exp, moe_w1, moe_w3, moe_w2, ln2_g, ln2_b)
    bp, tp, _ = x_prompt.shape
    bs, ts, _ = x_sample.shape
    assert ts == 1
    n_pages = page_table.shape[1]
    past = n_pages * PAGE_SIZE
    assert n_pages % SCORE_PAGES == 0 and n_pages % ATTEND_PAGES == 0

    xp = x_prompt.reshape(bp * tp, D_MODEL)

    def attend_p(qb, kf, vf, kb, vt, qib, kif, kib, wi):
        return _dsa_prompt(qib, wi.T, qb, kib, kb, vt, bp, tp)

    def scan_p(rw):
        s0 = jnp.zeros((bp, N_HEADS, HEAD_DIM, HEAD_DIM), F32)
        return _scan_chunked(rw, s0, pw["rk"], pw["rwlng"], pw["rwlnb"], bp, tp)

    y_p, k_p, v_p, ki_p, wkv_p = _layer(xp, None, _rope_tables(jnp.arange(tp)),
                                        pw, attend_p, scan_p, tm_proj=512, tm_post=512, tm_moe=1024)

    xs = x_sample.reshape(bs, D_MODEL)

    kidx_t = jnp.transpose(cache_kidx, (0, 2, 1))
    k_t = jnp.transpose(cache_k, (0, 2, 3, 1))
    v_t = jnp.transpose(cache_v, (0, 2, 3, 1))

    def attend_s(qb, kf, vf, kb, vt, qib, kif, kib, wi):
        scores, score_new = _sample_scores(page_table, qib, wi, kif.astype(cache_kidx.dtype), kidx_t, bs, n_pages)
        sel = _sample_select(scores.reshape(bs, past), score_new.reshape(bs, LANES), past)
        return _sample_attend(page_table, qb, sel, kf.astype(cache_k.dtype), vf.astype(cache_v.dtype),
                              k_t, v_t, bs, n_pages)

    def scan_s(rw):
        return _scan_step(rw, state_wkv.astype(F32), pw["rk"], pw["rwlng"], pw["rwlnb"], bs)

    y_s, k_s, v_s, ki_s, wkv_s = _layer(xs, state_shift.astype(x_sample.dtype),
                                        _rope_tables(jnp.full((bs,), past, I32)),
                                        pw, attend_s, scan_s, tm_proj=256, tm_post=512, tm_moe=1024)

    heads = lambda z, b, t: z.reshape(b, t, N_HEADS, HEAD_DIM)
    heads_t = lambda z: z.reshape(bp, N_HEADS, HEAD_DIM, tp).transpose(0, 3, 1, 2)
    return (y_p.reshape(bp, tp, D_MODEL), y_s.reshape(bs, ts, D_MODEL),
            heads_t(k_p).astype(cache_k.dtype), heads(v_p, bp, tp).astype(cache_v.dtype),
            ki_p.transpose(0, 2, 1).astype(cache_kidx.dtype),
            wkv_p.astype(state_wkv.dtype), x_prompt[:, -1].astype(state_shift.dtype),
            heads(k_s, bs, ts).astype(cache_k.dtype), heads(v_s, bs, ts).astype(cache_v.dtype),
            ki_s.reshape(bs, ts, IDX_DIM).astype(cache_kidx.dtype),
            wkv_s.astype(state_wkv.dtype), x_sample[:, -1].astype(state_shift.dtype))
```

```python
import functools
import math

import jax
import jax.numpy as jnp
from jax import lax
from jax.experimental import pallas as pl
from jax.experimental.pallas import tpu as pltpu

F32 = jnp.float32
BF16 = jnp.bfloat16
I32 = jnp.int32

D_MODEL = 1024
HEAD_DIM = 64
D_RWKV = 512
D_ATT = 512
N_HEADS = 8
LORA_W = 64
LORA_A = 64
LORA_G = 128
N_IDX_HEADS = 8
IDX_DIM = 64
TOPK_MAX = 256
ROPE_THETA = 500000.0
PAGE_SIZE = 128
N_GROUPS = 4
EXPERTS_PER_GROUP = 8
N_EXPERTS = 32
D_EXPERT = 256
LN_EPS = 1e-5
GN_EPS = HEAD_DIM * 1e-5
DECAY_SCALE = math.exp(-0.5)
ALPHA = 2.0 ** 0.25
NEG = -1e30
RW_COLS = 3 * D_RWKV + LORA_W + LORA_A + LORA_G
WI_SCALE = N_IDX_HEADS ** -0.5 * IDX_DIM ** -0.5
QK_SCALE = HEAD_DIM ** -0.5 * math.log2(math.e)
INT_MIN = -(2 ** 31)
INT_MAX = 2 ** 31 - 1

LANES = 128
SUBLANES = 8
VMEM_LIMIT = 56 * 1024 * 1024
SCAN_CHUNK = 64
SCAN_STEP_CHUNKS = 4
SOLVE_BLOCK = 16
EXPERT_CHUNK = 4
DSA_QUERIES = 512
DSA_KEYS = 512
ACC_ROWS = HEAD_DIM + 16
SCORE_PAGES = 16
ATTEND_PAGES = 8


def _dot(a, b):
    return jnp.dot(a, b, preferred_element_type=F32)


def _dot_nt(a, b):
    return lax.dot_general(a, b, (((1,), (1,)), ((), ())), preferred_element_type=F32)


def _dot_hi(a, b):
    return jnp.dot(a, b, preferred_element_type=F32, precision=lax.Precision.HIGHEST)


def _split(x):
    hi = x.astype(BF16)
    return hi, (x - hi.astype(F32)).astype(BF16)


def _dot3(a, b, nt=False):
    f = _dot_nt if nt else _dot
    ah, al = _split(a)
    bh, bl = _split(b)
    return f(ah, bh) + (f(ah, bl) + f(al, bh))


def _dot1(a, b, nt=False):
    return (_dot_nt if nt else _dot)(a.astype(BF16), b.astype(BF16))


def _params(*sem):
    return pltpu.CompilerParams(dimension_semantics=sem, vmem_limit_bytes=VMEM_LIMIT)


def _full(shape):
    nd = len(shape)
    return pl.BlockSpec(shape, lambda *_: (0,) * nd, pipeline_mode=pl.Buffered(1))


def _rope(x, cc, s1, s2):
    outs = []
    for c in range(x.shape[1] // LANES):
        xc = x[:, c * LANES:(c + 1) * LANES]
        outs.append(xc * cc + pltpu.roll(xc, LANES - 8, 1) * s1 + pltpu.roll(xc, 8, 1) * s2)
    return outs[0] if len(outs) == 1 else jnp.concatenate(outs, axis=1)


def _proj_body(x_ref, *refs, tiles_per_seq):
    if tiles_per_seq:
        (wrw_ref, watt_ref, wtail_ref, mu_ref, w0_ref, wup_ref, a0_ref, aup_ref, gup_ref, kk_ref, ka_ref,
         lng_ref, lnb_ref, cc_ref, s1_ref, s2_ref, r_o, lw_o, k_o, v_o, kkr_o, a_o, g_o, qb_o, kf_o, vf_o, kb_o,
         vt_o, qib_o, kif_o, kib_o, wi_o, carry_scr) = refs
    else:
        (xs_ref, wrw_ref, watt_ref, wtail_ref, mu_ref, w0_ref, wup_ref, a0_ref, aup_ref, gup_ref, kk_ref, ka_ref,
         lng_ref, lnb_ref, cc_ref, s1_ref, s2_ref, r_o, lw_o, k_o, v_o, kkr_o, a_o, g_o, qb_o, kf_o, vf_o, kb_o,
         vt_o, qib_o, kif_o, kib_o, wi_o) = refs
        xsb = xs_ref[...].astype(BF16)
    xb = x_ref[...].astype(BF16)
    tm = x_ref.shape[0]
    if tiles_per_seq:
        first_row = lax.broadcasted_iota(I32, (tm, 1), 0) == 0

        @pl.when(pl.program_id(0) % tiles_per_seq == 0)
        def _():
            carry_scr[...] = jnp.zeros_like(carry_scr)

    def lerp(c0, c1):
        w = wrw_ref[:, c0:c1]
        cur = _dot(xb, w)
        if tiles_per_seq:
            prev = jnp.where(first_row, carry_scr[0:1, c0:c1], pltpu.roll(cur, 1, 0))
            carry_scr[0:1, c0:c1] = cur[tm - 1:tm, :]
        else:
            prev = _dot(xsb, w)
        return cur + mu_ref[:, c0:c1] * (prev - cur)

    r_o[...] = lerp(0, D_RWKV)
    k = lerp(D_RWKV, 2 * D_RWKV)
    v_o[...] = lerp(2 * D_RWKV, 3 * D_RWKV)
    lora = lerp(3 * D_RWKV, 3 * D_RWKV + LORA_W + LORA_A)
    dg = lerp(3 * D_RWKV + LORA_W + LORA_A, RW_COLS)
    lw_o[...] = -DECAY_SCALE * jax.nn.sigmoid(w0_ref[...] + _dot(jnp.tanh(lora).astype(BF16), wup_ref[...]))
    a = jax.nn.sigmoid(a0_ref[...] + _dot(lora.astype(BF16), aup_ref[...]))
    a_o[...] = a
    g_o[...] = _dot(jax.nn.sigmoid(dg).astype(BF16), gup_ref[...])
    kkr_o[...] = k * kk_ref[...]
    k_o[...] = k * (1.0 + (a - 1.0) * ka_ref[...])

    cc, s1, s2 = cc_ref[...], s1_ref[...], s2_ref[...]
    q = _rope(_dot(xb, watt_ref[:, 0:D_ATT]), cc, s1, s2)
    qb_o[...] = (q * QK_SCALE).astype(BF16)
    ka = _rope(_dot(xb, watt_ref[:, D_ATT:2 * D_ATT]), cc, s1, s2)
    kb_o[...] = ka.astype(BF16)
    if tiles_per_seq:
        kf_o[0] = ka.T
    else:
        kf_o[...] = ka
    va = _dot(xb, watt_ref[:, 2 * D_ATT:3 * D_ATT])
    vf_o[...] = va
    vt_o[...] = va.T.astype(BF16)
    qi = _rope(_dot(xb, watt_ref[:, 3 * D_ATT:4 * D_ATT]), cc, s1, s2)
    qib_o[...] = qi.astype(BF16)

    tail = _dot(xb, wtail_ref[...])
    t0 = tail[:, :LANES]
    in_ki = lax.broadcasted_iota(I32, (1, LANES), 1) < IDX_DIM
    mean = jnp.sum(jnp.where(in_ki, t0, 0.0), axis=1, keepdims=True) * (1.0 / IDX_DIM)
    d = jnp.where(in_ki, t0 - mean, 0.0)
    var = jnp.sum(d * d, axis=1, keepdims=True) * (1.0 / IDX_DIM)
    ki_wide = _rope(d * lax.rsqrt(var + LN_EPS) * lng_ref[...] + lnb_ref[...], cc, s1, s2)
    ki = ki_wide[:, :IDX_DIM]
    if tiles_per_seq:
        kif_o[0] = ki_wide.T[:IDX_DIM, :]
    else:
        kif_o[...] = ki
    kib_o[...] = ki.astype(BF16)
    wi_o[...] = tail[:, LANES:LANES + N_IDX_HEADS] * WI_SCALE


def _rope_tables(pos):
    half = HEAD_DIM // 8
    inv = ROPE_THETA ** (-jnp.arange(half, dtype=F32) / half)
    ang = pos.astype(F32)[:, None] * inv[None, :]
    cos, sin = jnp.cos(ang), jnp.sin(ang)
    n = pos.shape[0]
    one = jnp.ones((n, HEAD_DIM - 2 * half), F32)
    zero = jnp.zeros((n, HEAD_DIM - 2 * half), F32)
    zh = jnp.zeros((n, half), F32)
    cc = jnp.concatenate([cos, cos, one], 1)
    s1 = jnp.concatenate([-sin, zh, zero], 1)
    s2 = jnp.concatenate([zh, sin, zero], 1)
    return tuple(jnp.tile(t, (1, LANES // HEAD_DIM)) for t in (cc, s1, s2))


def _proj(x, xs, tables, pw, tm):
    n = x.shape[0]
    nt = tables[0].shape[0] // tm
    tiles_per_seq = nt if xs is None else 0
    row = lambda w: pl.BlockSpec((tm, w), lambda i: (i, 0))
    tab = pl.BlockSpec((tm, LANES), lambda i: (i % nt, 0))
    weights = (pw["wrw"], pw["watt"], pw["wtail"], pw["mu"], pw["w0"], pw["wup"], pw["a0"], pw["aup"],
               pw["gup"], pw["kk"], pw["ka"], pw["lng"], pw["lnb"])
    f = lambda w, dt=F32: jax.ShapeDtypeStruct((n, w), dt)
    if tiles_per_seq:
        cache_shape = lambda w: jax.ShapeDtypeStruct((n // (nt * tm), w, nt * tm), F32)
        cache_spec = lambda w: pl.BlockSpec((1, w, tm), lambda i: (i // nt, 0, i % nt))
    else:
        cache_shape, cache_spec = f, row
    out_shape = ([f(D_RWKV)] * 7 + [f(D_ATT, BF16), cache_shape(D_ATT), f(D_ATT), f(D_ATT, BF16),
                                    jax.ShapeDtypeStruct((D_ATT, n), BF16),
                                    f(D_ATT, BF16), cache_shape(IDX_DIM), f(IDX_DIM, BF16), f(N_IDX_HEADS)])
    out_specs = ([row(D_RWKV)] * 7 + [row(D_ATT), cache_spec(D_ATT), row(D_ATT), row(D_ATT),
                                      pl.BlockSpec((D_ATT, tm), lambda i: (0, i)),
                                      row(D_ATT), cache_spec(IDX_DIM), row(IDX_DIM), row(N_IDX_HEADS)])
    rows_in = (x,) if xs is None else (x, xs)
    return pl.pallas_call(
        functools.partial(_proj_body, tiles_per_seq=tiles_per_seq),
        grid=(n // tm,),
        in_specs=[row(D_MODEL)] * len(rows_in) + [_full(w.shape) for w in weights] + [tab] * 3,
        out_specs=out_specs,
        out_shape=out_shape,
        scratch_shapes=[pltpu.VMEM((SUBLANES, RW_COLS), F32)] if xs is None else [],
        compiler_params=_params("arbitrary" if xs is None else "parallel"),
        name="proj",
    )(*rows_in, *weights, *tables)


def _head_epilogue(o, r, k, v, g, rk, lng, lnb):
    mu = jnp.mean(o, axis=-1, keepdims=True)
    d = o - mu
    var = jnp.mean(d * d, axis=-1, keepdims=True)
    on = d * lax.rsqrt(var + GN_EPS) * lng + lnb
    bonus = jnp.sum(r * k * rk, axis=-1, keepdims=True) * v
    return (on + bonus) * g


def _scan_chunk_body(r_ref, lw_ref, k_ref, v_ref, kk_ref, a_ref, g_ref, rk_ref, lng_ref, lnb_ref, s0_ref,
                     o_ref, sf_ref, s_scr):
    c = pl.program_id(1)

    @pl.when(c == 0)
    def _():
        s_scr[...] = s0_ref[0]

    C = SCAN_CHUNK
    heads = range(N_HEADS)
    row = lax.broadcasted_iota(I32, (C, 2 * C), 0)
    col = lax.broadcasted_iota(I32, (C, 2 * C), 1) % C
    strict2 = col < row
    incl2 = col <= row
    same_blk = (col // SOLVE_BLOCK) == (row // SOLVE_BLOCK)
    left = lax.broadcasted_iota(I32, (C, 2 * C), 1) < C
    tri = incl2[:, :C].astype(BF16)

    sls = [slice(h * HEAD_DIM, (h + 1) * HEAD_DIM) for h in heads]

    def chunk(ci, carry):
        rows = pl.ds(pl.multiple_of(ci * C, C), C)
        lw_all = lw_ref[rows, :]
        l_hi = lw_all.astype(BF16)
        l_r1 = lw_all - l_hi.astype(F32)
        l_mid = l_r1.astype(BF16)
        l_lo = (l_r1 - l_mid.astype(F32)).astype(BF16)
        cum_all = _dot(tri, l_hi) + (_dot(tri, l_mid) + _dot(tri, l_lo))

        r = [r_ref[rows, sl] for sl in sls]
        k = [k_ref[rows, sl] for sl in sls]
        v = [v_ref[rows, sl] for sl in sls]
        lhs1, rhs1, bk_end, g_end = [], [], [], []
        for h in heads:
            sl = sls[h]
            lw, kkr, a = lw_all[:, sl], kk_ref[rows, sl], a_ref[rows, sl]
            cum = cum_all[:, sl]
            cum_end = cum[C - 1:C, :]
            kk = kkr * lax.rsqrt(jnp.maximum(jnp.sum(kkr * kkr, axis=-1, keepdims=True), 1e-24))
            beta = kk * a
            e_neg = jnp.exp(-cum)
            e_end = jnp.exp(cum_end - cum)
            lhs1.append(jnp.concatenate([-kk * jnp.exp(cum - lw), r[h] * jnp.exp(cum)], axis=0))
            rhs1.append(jnp.concatenate([beta * e_neg, k[h] * e_neg, s_scr[h]], axis=0))
            bk_end.append(jnp.concatenate([beta * e_end, k[h] * e_end], axis=0))
            g_end.append(jnp.exp(cum_end))

        r1 = [_dot1(lhs1[h], rhs1[h], nt=True) for h in heads]
        vv = [jnp.concatenate([v[h], v[h]], axis=0) for h in heads]
        y = [r1[h][:C, 2 * C:] + _dot1(jnp.where(jnp.logical_and(strict2, ~left), r1[h][:C, :2 * C], 0.0), vv[h])
             for h in heads]
        n_pair = [r1[h][:C, :2 * C] for h in heads]
        p = [jnp.where(jnp.logical_and(strict2, same_blk), n_pair[h], 0.0)[:, :C] for h in heads]
        x = [jnp.where(left, jnp.where(jnp.logical_and(strict2, ~same_blk), n_pair[h], 0.0),
                       jnp.concatenate([y[h], y[h]], axis=1)) for h in heads]
        n_steps = SOLVE_BLOCK.bit_length() - 1
        for i in range(n_steps):
            if i + 1 < n_steps:
                rr = [_dot3(p[h], jnp.concatenate([x[h], p[h]], axis=1)) for h in heads]
                x = [x[h] + rr[h][:, :2 * C] for h in heads]
                p = [rr[h][:, 2 * C:] for h in heads]
            else:
                x = [x[h] + _dot3(p[h], x[h]) for h in heads]
        rr = [_dot3(x[h][:, :C], x[h]) for h in heads]
        x = [jnp.where(left, rr[h], x[h] + rr[h]) for h in heads]
        rr = [_dot3(x[h][:, :C], x[h]) for h in heads]
        u = [(x[h] + rr[h])[:, C:] for h in heads]

        uv = [jnp.concatenate([u[h], v[h]], axis=0) for h in heads]
        o = [r1[h][C:, 2 * C:] + _dot1(jnp.where(incl2, r1[h][C:, :2 * C], 0.0), uv[h]) for h in heads]
        for h in heads:
            s_new = rhs1[h][2 * C:, :] * g_end[h] + _dot3(uv[h].T, bk_end[h])
            s_scr[h] = s_new
            sl = sls[h]
            o_ref[rows, sl] = _head_epilogue(o[h], r[h], k[h], v[h], g_ref[rows, sl], rk_ref[:, sl],
                                             lng_ref[:, sl], lnb_ref[:, sl])
        return carry

    lax.fori_loop(0, r_ref.shape[0] // C, chunk, 0)

    @pl.when(c == pl.num_programs(1) - 1)
    def _():
        sf_ref[0] = s_scr[...]


def _scan_chunked(rw, s0, rk, lng, lnb, n_seq, t):
    rows = min(t, SCAN_CHUNK * SCAN_STEP_CHUNKS)
    nc = t // rows
    row = pl.BlockSpec((rows, D_RWKV), lambda b, c: (b * nc + c, 0))
    st = pl.BlockSpec((1, N_HEADS, HEAD_DIM, HEAD_DIM), lambda b, c: (b, 0, 0, 0))
    par = _full((1, D_RWKV))
    return pl.pallas_call(
        _scan_chunk_body,
        grid=(n_seq, nc),
        in_specs=[row] * 7 + [par] * 3 + [st],
        out_specs=[row, st],
        out_shape=[jax.ShapeDtypeStruct((n_seq * t, D_RWKV), F32),
                   jax.ShapeDtypeStruct((n_seq, N_HEADS, HEAD_DIM, HEAD_DIM), F32)],
        scratch_shapes=[pltpu.VMEM((N_HEADS, HEAD_DIM, HEAD_DIM), F32)],
        compiler_params=_params("parallel", "arbitrary"),
        name="scan_chunked",
    )(*rw, rk, lng, lnb, s0)


def _scan_step_body(r_ref, lw_ref, k_ref, v_ref, kk_ref, a_ref, g_ref, rk_ref, lng_ref, lnb_ref, s0_ref,
                    o_ref, sf_ref):
    eye = (lax.broadcasted_iota(I32, (HEAD_DIM, HEAD_DIM), 0)
           == lax.broadcasted_iota(I32, (HEAD_DIM, HEAD_DIM), 1)).astype(F32)
    outs = []
    for h in range(N_HEADS):
        sl = slice(h * HEAD_DIM, (h + 1) * HEAD_DIM)
        r, lw, k, v = r_ref[0, :, sl], lw_ref[0, :, sl], k_ref[0, :, sl], v_ref[0, :, sl]
        kkr, a = kk_ref[0, :, sl], a_ref[0, :, sl]
        kk = kkr * lax.rsqrt(jnp.maximum(jnp.sum(kkr * kkr, axis=-1, keepdims=True), 1e-24))
        s = s0_ref[0, h]
        sa = jnp.sum(s * kk, axis=1, keepdims=True)
        v_col = jnp.sum(eye * v, axis=1, keepdims=True)
        s = s * jnp.exp(lw) - sa * (kk * a) + v_col * k
        sf_ref[0, h] = s
        o_col = jnp.sum(s * r, axis=1, keepdims=True)
        o = jnp.sum(eye * o_col, axis=0, keepdims=True)
        outs.append(_head_epilogue(o, r, k, v, g_ref[0, :, sl], rk_ref[:, sl], lng_ref[:, sl], lnb_ref[:, sl]))
    o_ref[0] = jnp.concatenate(outs, axis=1)


def _scan_step(rw, s0, rk, lng, lnb, n_seq):
    row = pl.BlockSpec((1, 1, D_RWKV), lambda b: (b, 0, 0))
    st = pl.BlockSpec((1, N_HEADS, HEAD_DIM, HEAD_DIM), lambda b: (b, 0, 0, 0))
    par = _full((1, D_RWKV))
    rw3 = [z.reshape(n_seq, 1, D_RWKV) for z in rw]
    o, sf = pl.pallas_call(
        _scan_step_body,
        grid=(n_seq,),
        in_specs=[row] * 7 + [par] * 3 + [st],
        out_specs=[row, st],
        out_shape=[jax.ShapeDtypeStruct((n_seq, 1, D_RWKV), F32),
                   jax.ShapeDtypeStruct((n_seq, N_HEADS, HEAD_DIM, HEAD_DIM), F32)],
        compiler_params=_params("parallel"),
        name="scan_step",
    )(*rw3, rk, lng, lnb, s0)
    return o.reshape(n_seq, D_RWKV), sf


def _order_key(score):
    bits = pltpu.bitcast(score, I32)
    return bits ^ ((bits >> 31) & INT_MAX)


def _select_threshold(count_ge, count_tie_le, n_valid, ktop, idx_bits):
    shape = n_valid.shape
    lo = jnp.full(shape, INT_MIN + 1, I32)
    hi = jnp.where(n_valid <= ktop, lo + 1, INT_MAX)

    def cond(st):
        it, lo, hi, _ = st
        return jnp.logical_and(it < 33, jnp.max(jnp.where(lo + 1 != hi, 1.0, 0.0)) > 0.0)

    def body(st):
        it, lo, hi, c_lo = st
        mid = (lo & hi) + ((lo ^ hi) >> 1)
        cnt = count_ge(mid)
        live = lo + 1 != hi
        up = jnp.logical_and(live, cnt >= ktop)
        down = jnp.logical_and(live, cnt < ktop)
        hi2 = jnp.where(jnp.logical_and(up, cnt == ktop), mid + 1, jnp.where(down, mid, hi))
        return it + 1, jnp.where(up, mid, lo), hi2, jnp.where(up, cnt, c_lo)

    _, thr, _, c_thr = lax.while_loop(cond, body, (jnp.int32(0), lo, hi, n_valid.astype(F32)))
    excess = jnp.logical_and(c_thr > ktop, n_valid > ktop)
    has_tie = jnp.max(jnp.where(excess, 1.0, 0.0)) > 0.0
    all_idx = jnp.full(shape, (1 << idx_bits) - 1, I32)

    def tie_search():
        need = ktop - count_ge(thr + 1)

        def tbody(_, st):
            lo_j, hi_j = st
            mid = (lo_j + hi_j) >> 1
            ok = count_tie_le(thr, mid) >= need
            return jnp.where(ok, lo_j, mid + 1), jnp.where(ok, mid, hi_j)

        lo_j, _ = lax.fori_loop(0, idx_bits, tbody, (jnp.zeros(shape, I32), all_idx))
        return jnp.where(excess, lo_j, all_idx)

    tie_end = lax.cond(has_tie, tie_search, lambda: all_idx)
    return thr, tie_end, has_tie


def _dsa_prompt_body(qi_ref, wit_ref, q_ref, ki_ref, k_ref, vt_ref, o_ref, key_scr, m_scr, bias_scr, s_scr, acc_scr,
                     *, ktop, tq, tk, idx_bits):
    i = pl.program_id(1)
    n_kt = (i + 1) * (tq // tk)
    qpos = i * tq + lax.broadcasted_iota(I32, (1, tq), 1)
    key_off = lax.broadcasted_iota(I32, (tk, 1), 0)
    tile = lambda kt: pl.ds(pl.multiple_of(kt * tk, tk), tk)
    qi = qi_ref[...]
    qi_h = [qi[:, h * IDX_DIM:(h + 1) * IDX_DIM] for h in range(N_IDX_HEADS)]
    wit = wit_ref[...]

    def score_tile(kt, carry):
        kis = ki_ref[tile(kt), :]
        acc = jnp.zeros((tk, tq), F32)
        for h in range(N_IDX_HEADS):
            acc = acc + jnp.maximum(_dot_nt(kis, qi_h[h]), 0.0) * wit[h:h + 1, :]
        key_scr[tile(kt), :] = jnp.where(kt * tk + key_off <= qpos, _order_key(acc), INT_MIN)
        return carry

    lax.fori_loop(0, n_kt, score_tile, 0)

    def count(pred):
        def body(kt, acc):
            hit = jnp.where(pred(key_scr[tile(kt), :], kt * tk + key_off), 1.0, 0.0)
            return acc + jnp.sum(hit.reshape(tk // SUBLANES, SUBLANES, tq), axis=0)
        acc = lax.fori_loop(0, n_kt, body, jnp.zeros((SUBLANES, tq), F32))
        return jnp.sum(acc, axis=0, keepdims=True)

    thr, tie_end, has_tie = _select_threshold(
        lambda t: count(lambda keys, pos: keys >= t),
        lambda t, j: count(lambda keys, pos: jnp.logical_and(keys == t, pos <= j)),
        qpos + 1, ktop, idx_bits)

    @pl.when(has_tie)
    def _():
        def demote(kt, carry):
            keys = key_scr[tile(kt), :]
            late = jnp.logical_and(keys == thr, kt * tk + key_off > tie_end)
            key_scr[tile(kt), :] = jnp.where(late, thr - 1, keys)
            return carry
        lax.fori_loop(0, n_kt, demote, 0)

    q = q_ref[...]
    lane = lax.broadcasted_iota(I32, (1, LANES), 1)
    q_pad = []
    for h in range(N_HEADS):
        own = (lane < HEAD_DIM) if h % 2 == 0 else (lane >= HEAD_DIM)
        q_pad.append(jnp.where(own, q[:, (h // 2) * LANES:(h // 2 + 1) * LANES].astype(F32), 0.0).astype(BF16))
    m_scr[...] = jnp.full(m_scr.shape, NEG, F32)
    acc_scr[...] = jnp.zeros(acc_scr.shape, F32)
    ones = jnp.ones((ACC_ROWS - HEAD_DIM, tk), BF16)

    def attend(kt, carry):
        bias_scr[...] = jnp.where(key_scr[tile(kt), :] >= thr, 0.0, NEG)
        tile_max = []
        for h in range(N_HEADS):
            s = _dot_nt(k_ref[tile(kt), (h // 2) * LANES:(h // 2 + 1) * LANES], q_pad[h]) + bias_scr[...]
            s_scr[h] = s
            tile_max.append(jnp.max(s, axis=0, keepdims=True))
        m_old = m_scr[...]
        m_new = jnp.maximum(m_old, jnp.concatenate(tile_max, axis=0))
        alpha = jnp.exp2(m_old - m_new)
        m_scr[...] = m_new
        for h in range(N_HEADS):
            p = jnp.exp2(s_scr[h] - m_new[h:h + 1, :]).astype(BF16)
            v_ext = jnp.concatenate([vt_ref[h * HEAD_DIM:(h + 1) * HEAD_DIM, tile(kt)], ones], axis=0)
            rows = slice(h * ACC_ROWS, (h + 1) * ACC_ROWS)
            acc_scr[rows, :] = alpha[h:h + 1, :] * acc_scr[rows, :] + _dot(v_ext, p)
        return carry

    lax.fori_loop(0, n_kt, attend, 0)
    out_t = jnp.concatenate(
        [acc_scr[h * ACC_ROWS:h * ACC_ROWS + HEAD_DIM, :] / acc_scr[h * ACC_ROWS + HEAD_DIM:h * ACC_ROWS + HEAD_DIM + 1, :]
         for h in range(N_HEADS)], axis=0)
    o_ref[...] = out_t.T


def _dsa_prompt(qib, wit, qb, kib, kb, vt, n_seq, t):
    tq = min(t, DSA_QUERIES)
    tk = min(t, DSA_KEYS)
    nq = t // tq
    ktop = min(TOPK_MAX, t // 4)
    qrow = lambda w: pl.BlockSpec((tq, w), lambda b, i: (b * nq + i, 0))
    seq = lambda w: pl.BlockSpec((t, w), lambda b, i: (b, 0), pipeline_mode=pl.Buffered(1))
    body = functools.partial(_dsa_prompt_body, ktop=ktop, tq=tq, tk=tk, idx_bits=max(1, (t - 1).bit_length()))
    return pl.pallas_call(
        body,
        grid=(n_seq, nq),
        in_specs=[qrow(D_ATT), pl.BlockSpec((N_IDX_HEADS, tq), lambda b, i: (0, b * nq + i)), qrow(D_ATT),
                  seq(IDX_DIM), seq(D_ATT),
                  pl.BlockSpec((D_ATT, t), lambda b, i: (0, b), pipeline_mode=pl.Buffered(1))],
        out_specs=qrow(D_ATT),
        out_shape=jax.ShapeDtypeStruct((n_seq * t, D_ATT), F32),
        scratch_shapes=[pltpu.VMEM((t, tq), I32), pltpu.VMEM((N_HEADS, tq), F32), pltpu.VMEM((tk, tq), F32),
                        pltpu.VMEM((N_HEADS, tk, tq), F32), pltpu.VMEM((N_HEADS * ACC_ROWS, tq), F32)],
        compiler_params=_params("parallel", "arbitrary"),
        name="dsa_prompt",
    )(qib, wit, qb, kib, kb, vt)


def _sample_score_body(pt_ref, qi_ref, wi_ref, kin_ref, *rest):
    page_refs, (sc_ref, scn_ref) = rest[:SCORE_PAGES], rest[SCORE_PAGES:]
    qi = qi_ref[0]
    wi_col = wi_ref[0]
    pages = jnp.concatenate([p[0] for p in page_refs], axis=1).astype(BF16)
    rel = jnp.maximum(_dot(qi, pages), 0.0) * wi_col
    sc_ref[0] = 0.0 + jnp.sum(rel, axis=0, keepdims=True)

    @pl.when(pl.program_id(1) == 0)
    def _():
        kn = kin_ref[0].astype(BF16).astype(F32)
        rel_n = jnp.maximum(jnp.sum(qi.astype(F32) * kn, axis=1, keepdims=True), 0.0) * wi_col
        scn_ref[0] = jnp.broadcast_to(0.0 + jnp.sum(rel_n, axis=0, keepdims=True), (1, LANES))


def _sample_scores(page_table, qib, wi, ki_new, kidx_t, n_seq, n_pages):
    g = SCORE_PAGES
    page = lambda s: pl.BlockSpec((1, IDX_DIM, PAGE_SIZE), lambda b, j, pt: (pt[b * n_pages + j * g + s], 0, 0))
    grid_spec = pltpu.PrefetchScalarGridSpec(
        num_scalar_prefetch=1,
        grid=(n_seq, n_pages // g),
        in_specs=[pl.BlockSpec((1, N_IDX_HEADS, IDX_DIM), lambda b, j, pt: (b, 0, 0)),
                  pl.BlockSpec((1, N_IDX_HEADS, 1), lambda b, j, pt: (b, 0, 0)),
                  pl.BlockSpec((1, 1, IDX_DIM), lambda b, j, pt: (b, 0, 0))] + [page(s) for s in range(g)],
        out_specs=[pl.BlockSpec((1, 1, g * PAGE_SIZE), lambda b, j, pt: (b, 0, j)),
                   pl.BlockSpec((1, 1, LANES), lambda b, j, pt: (b, 0, 0))])
    return pl.pallas_call(
        _sample_score_body,
        grid_spec=grid_spec,
        out_shape=[jax.ShapeDtypeStruct((n_seq, 1, n_pages * PAGE_SIZE), F32),
                   jax.ShapeDtypeStruct((n_seq, 1, LANES), F32)],
        compiler_params=_params("parallel", "arbitrary"),
        name="sample_scores",
    )(page_table.reshape(-1), qib.reshape(n_seq, N_IDX_HEADS, IDX_DIM), wi.reshape(n_seq, N_IDX_HEADS, 1),
      ki_new.reshape(n_seq, 1, IDX_DIM), *([kidx_t] * g))


def _sample_select_body(sc_ref, scn_ref, sel_ref, key_scr, *, ktop, past, idx_bits):
    rows = sc_ref.shape[0]
    n_t = past // LANES + 1
    lane_pos = lax.broadcasted_iota(I32, (1, LANES), 1)
    tile = lambda kt: pl.ds(pl.multiple_of(kt * LANES, LANES), LANES)
    key_scr[:, :past] = _order_key(sc_ref[...])
    key_scr[:, past:] = jnp.where(lane_pos == 0, _order_key(scn_ref[...]), INT_MIN)

    def count(pred):
        def body(kt, acc):
            return acc + jnp.where(pred(key_scr[:, tile(kt)], kt * LANES + lane_pos), 1.0, 0.0)
        acc = lax.fori_loop(0, n_t, body, jnp.zeros((rows, LANES), F32))
        return jnp.sum(acc, axis=1, keepdims=True)

    thr, tie_end, _ = _select_threshold(
        lambda t: count(lambda keys, pos: keys >= t),
        lambda t, j: count(lambda keys, pos: jnp.logical_and(keys == t, pos <= j)),
        jnp.full((rows, 1), past + 1, I32), ktop, idx_bits)

    def write(kt, carry):
        keys = key_scr[:, tile(kt)]
        sel = jnp.logical_or(keys > thr, jnp.logical_and(keys >= thr, kt * LANES + lane_pos <= tie_end))
        sel_ref[:, tile(kt)] = jnp.where(sel, 1.0, 0.0)
        return carry

    lax.fori_loop(0, n_t, write, 0)


def _sample_select(scores, score_new, past):
    rows = scores.shape[0]
    width = past + LANES
    ktop = min(TOPK_MAX, (past + 1) // 4)
    body = functools.partial(_sample_select_body, ktop=ktop, past=past, idx_bits=width.bit_length())
    rb = min(rows, SUBLANES)
    blk = lambda w: pl.BlockSpec((rb, w), lambda i: (i, 0))
    return pl.pallas_call(
        body,
        grid=(rows // rb,),
        in_specs=[blk(past), blk(LANES)],
        out_specs=blk(width),
        out_shape=jax.ShapeDtypeStruct((rows, width), F32),
        scratch_shapes=[pltpu.VMEM((rb, width), I32)],
        compiler_params=_params("parallel"),
        name="sample_select",
    )(scores, score_new)


def _sample_attend_body(pt_ref, q_ref, sel_ref, seln_ref, kn_ref, vn_ref, *rest):
    g = ATTEND_PAGES
    kp_refs, vp_refs, (o_ref, m_scr, l_scr, acc_scr) = rest[:g], rest[g:2 * g], rest[2 * g:]
    head_tokens = lambda refs, h: jnp.concatenate([r[0, h] for r in refs], axis=1).astype(BF16)
    j = pl.program_id(1)
    q = q_ref[0]
    row = lax.broadcasted_iota(I32, (N_HEADS, 1), 0)

    @pl.when(j == 0)
    def _():
        kn = kn_ref[0].astype(BF16).astype(F32)
        logit = jnp.sum(q.astype(F32) * kn, axis=1, keepdims=True)
        on = seln_ref[0][:, 0:1] > 0.0
        m_scr[...] = jnp.where(on, logit, NEG)
        l_scr[...] = jnp.where(on, jnp.ones_like(logit), 0.0)
        acc_scr[...] = jnp.where(on, vn_ref[0].astype(BF16).astype(F32), 0.0)

    on = sel_ref[0] > 0.0
    s = jnp.zeros((N_HEADS, g * PAGE_SIZE), F32)
    for h in range(N_HEADS):
        s = jnp.where(row == h, _dot(q, head_tokens(kp_refs, h)), s)
    s = jnp.where(on, s, NEG)
    m_old = m_scr[...]
    m_new = jnp.maximum(m_old, jnp.max(s, axis=1, keepdims=True))
    alpha = jnp.exp2(m_old - m_new)
    p = jnp.where(on, jnp.exp2(s - m_new), 0.0)
    m_scr[...] = m_new
    l_scr[...] = alpha * l_scr[...] + jnp.sum(p, axis=1, keepdims=True)
    pb = p.astype(BF16)
    acc = alpha * acc_scr[...]
    for h in range(N_HEADS):
        acc = acc + jnp.where(row == h, _dot_nt(pb, head_tokens(vp_refs, h)), 0.0)
    acc_scr[...] = acc

    @pl.when(j == pl.num_programs(1) - 1)
    def _():
        o_ref[0] = acc_scr[...] / l_scr[...]


def _sample_attend(page_table, qb, sel, k_new, v_new, k_t, v_t, n_seq, n_pages):
    g = ATTEND_PAGES
    sel3 = sel.reshape(n_seq, 1, (n_pages + 1) * PAGE_SIZE)
    head = lambda: pl.BlockSpec((1, N_HEADS, HEAD_DIM), lambda b, j, pt: (b, 0, 0))
    page = lambda s: pl.BlockSpec((1, N_HEADS, HEAD_DIM, PAGE_SIZE),
                                  lambda b, j, pt: (pt[b * n_pages + j * g + s], 0, 0, 0))
    grid_spec = pltpu.PrefetchScalarGridSpec(
        num_scalar_prefetch=1,
        grid=(n_seq, n_pages // g),
        in_specs=[head(),
                  pl.BlockSpec((1, 1, g * PAGE_SIZE), lambda b, j, pt: (b, 0, j)),
                  pl.BlockSpec((1, 1, PAGE_SIZE), lambda b, j, pt: (b, 0, n_pages)),
                  head(), head()] + [page(s) for s in range(g)] * 2,
        out_specs=head(),
        scratch_shapes=[pltpu.VMEM((N_HEADS, 1), F32), pltpu.VMEM((N_HEADS, 1), F32),
                        pltpu.VMEM((N_HEADS, HEAD_DIM), F32)])
    to_heads = lambda z: z.reshape(n_seq, N_HEADS, HEAD_DIM)
    out = pl.pallas_call(
        _sample_attend_body,
        grid_spec=grid_spec,
        out_shape=jax.ShapeDtypeStruct((n_seq, N_HEADS, HEAD_DIM), F32),
        compiler_params=_params("parallel", "arbitrary"),
        name="sample_attend",
    )(page_table.reshape(-1), to_heads(qb), sel3, sel3, to_heads(k_new), to_heads(v_new),
      *([k_t] * g), *([v_t] * g))
    return out.reshape(n_seq, D_ATT)


def _layer_norm(x, g, b):
    mu = jnp.mean(x, axis=-1, keepdims=True)
    d = x - mu
    var = jnp.mean(d * d, axis=-1, keepdims=True)
    return d * lax.rsqrt(var + LN_EPS) * g + b


def _post_body(x_ref, orw_ref, oatt_ref, wo_ref, g_ref, b_ref, wr_ref, h_o, gate_o):
    mix = _dot(orw_ref[...].astype(BF16), wo_ref[0:D_RWKV, :]) + _dot(oatt_ref[...].astype(BF16), wo_ref[D_RWKV:, :])
    h = _layer_norm(ALPHA * x_ref[...] + mix, g_ref[...], b_ref[...])
    h_o[...] = h
    logit = _dot_hi(h, wr_ref[...])
    rows = logit.shape[0]
    lane = lax.broadcasted_iota(I32, (rows, LANES), 1).astype(F32)
    big = float(LANES)

    def first_max(mask):
        val = jnp.max(jnp.where(mask, logit, -jnp.inf), axis=1, keepdims=True)
        idx = jnp.min(jnp.where(jnp.logical_and(mask, logit == val), lane, big), axis=1, keepdims=True)
        return val, idx

    is_grp = jnp.logical_and(lane >= N_EXPERTS, lane < N_EXPERTS + N_GROUPS)
    g_max, g_lane = first_max(is_grp)
    p_grp = 1.0 / jnp.sum(jnp.where(is_grp, jnp.exp(logit - g_max), 0.0), axis=1, keepdims=True)
    in_grp = jnp.floor(lane * (1.0 / EXPERTS_PER_GROUP)) == (g_lane - N_EXPERTS)
    v1, i1 = first_max(in_grp)
    v2, i2 = first_max(jnp.logical_and(in_grp, lane != i1))
    e2 = jnp.exp(v2 - v1)
    g1 = p_grp / (1.0 + e2)
    g2 = p_grp * e2 / (1.0 + e2)
    gate_o[...] = jnp.where(lane == i1, g1, 0.0) + jnp.where(lane == i2, g2, 0.0)


def _post(x, o_rw, o_att, pw, tm):
    n = x.shape[0]
    row = lambda w: pl.BlockSpec((tm, w), lambda i: (i, 0))
    return pl.pallas_call(
        _post_body,
        grid=(n // tm,),
        in_specs=[row(D_MODEL), row(D_RWKV), row(D_ATT), _full((D_MODEL, D_MODEL)), _full((1, D_MODEL)),
                  _full((1, D_MODEL)), _full((D_MODEL, LANES))],
        out_specs=[row(D_MODEL), row(LANES)],
        out_shape=[jax.ShapeDtypeStruct((n, D_MODEL), F32), jax.ShapeDtypeStruct((n, LANES), F32)],
        compiler_params=_params("parallel"),
        name="post",
    )(x, o_rw, o_att, pw["wout"], pw["ln1g"], pw["ln1b"], pw["wrouter"])


def _moe_body(h_ref, gate_ref, w1_ref, w3_ref, w2_ref, g_ref, b_ref, y_ref, hb_scr, acc_scr):
    e = pl.program_id(1)

    @pl.when(e == 0)
    def _():
        hb_scr[...] = h_ref[...].astype(BF16)
        acc_scr[...] = jnp.zeros_like(acc_scr)

    hb = hb_scr[...]
    gates = gate_ref[0]
    for j in range(EXPERT_CHUNK):
        h1 = _dot(hb, w1_ref[j])
        h3 = _dot(hb, w3_ref[j])
        hidden = jax.nn.silu(h1) * h3 * gates[:, j:j + 1]
        acc_scr[...] += _dot(hidden.astype(BF16), w2_ref[j])

    @pl.when(e == pl.num_programs(1) - 1)
    def _():
        y_ref[...] = _layer_norm(ALPHA * h_ref[...] + acc_scr[...], g_ref[...], b_ref[...])


def _moe(h, gate, pw, tm):
    n = h.shape[0]
    n_ec = N_EXPERTS // EXPERT_CHUNK
    gate_c = gate[:, :N_EXPERTS].reshape(n, n_ec, EXPERT_CHUNK).transpose(1, 0, 2)
    return pl.pallas_call(
        _moe_body,
        grid=(n // tm, n_ec),
        in_specs=[pl.BlockSpec((tm, D_MODEL), lambda i, e: (i, 0)),
                  pl.BlockSpec((1, tm, EXPERT_CHUNK), lambda i, e: (e, i, 0)),
                  pl.BlockSpec((EXPERT_CHUNK, D_MODEL, D_EXPERT), lambda i, e: (e, 0, 0)),
                  pl.BlockSpec((EXPERT_CHUNK, D_MODEL, D_EXPERT), lambda i, e: (e, 0, 0)),
                  pl.BlockSpec((EXPERT_CHUNK, D_EXPERT, D_MODEL), lambda i, e: (e, 0, 0)),
                  _full((1, D_MODEL)), _full((1, D_MODEL))],
        out_specs=pl.BlockSpec((tm, D_MODEL), lambda i, e: (i, 0)),
        out_shape=jax.ShapeDtypeStruct((n, D_MODEL), F32),
        scratch_shapes=[pltpu.VMEM((tm, D_MODEL), BF16), pltpu.VMEM((tm, D_MODEL), F32)],
        compiler_params=_params("parallel", "arbitrary"),
        name="moe",
    )(h, gate_c, pw["w1"], pw["w3"], pw["w2"], pw["ln2g"], pw["ln2b"])


def _prepare_weights(w_in, rw_mu, rw_w0, rw_w_up, rw_a0, rw_a_up, rw_g_up, rw_k_k, rw_k_a, rw_r_k,
                     rw_ln_g, rw_ln_b, idx_ln_g, idx_ln_b, w_out, ln1_g, ln1_b,
                     moe_w_grp, moe_w_exp, moe_w1, moe_w3, moe_w2, ln2_g, ln2_b):
    row = lambda z: z.reshape(1, -1).astype(F32)
    c5 = RW_COLS + 4 * D_ATT
    pad_to = lambda z, w: jnp.pad(z, ((0, 0), (0, w - z.shape[1])))
    wtail = jnp.concatenate([pad_to(w_in[:, c5:c5 + IDX_DIM], LANES),
                             pad_to(w_in[:, c5 + IDX_DIM:], LANES)], axis=1)
    zeros_w = jnp.zeros((LORA_W, D_RWKV), F32)
    return {
        "wrw": w_in[:, :RW_COLS].astype(BF16),
        "watt": w_in[:, RW_COLS:c5].astype(BF16),
        "wtail": wtail.astype(BF16),
        "mu": row(rw_mu), "w0": row(rw_w0), "a0": row(rw_a0), "kk": row(rw_k_k), "ka": row(rw_k_a),
        "wup": jnp.concatenate([rw_w_up, zeros_w], 0).astype(BF16),
        "aup": jnp.concatenate([zeros_w, rw_a_up], 0).astype(BF16),
        "gup": rw_g_up.astype(BF16),
        "lng": pad_to(row(idx_ln_g), LANES), "lnb": pad_to(row(idx_ln_b), LANES),
        "rk": row(rw_r_k), "rwlng": row(rw_ln_g), "rwlnb": row(rw_ln_b),
        "wout": w_out.astype(BF16), "ln1g": row(ln1_g), "ln1b": row(ln1_b),
        "wrouter": pad_to(jnp.concatenate([moe_w_exp, moe_w_grp], 1).astype(F32), LANES),
        "w1": moe_w1.astype(BF16), "w3": moe_w3.astype(BF16), "w2": moe_w2.astype(BF16),
        "ln2g": row(ln2_g), "ln2b": row(ln2_b),
    }


def _row_tile(n, want):
    return min(n, want)


def _layer(x, x_shift, pos_tables, pw, attend, scan, *, tm_proj, tm_post, tm_moe):
    n = x.shape[0]
    outs = _proj(x, x_shift, pos_tables, pw, _row_tile(n, tm_proj))
    rw = outs[:7]
    qb, kf, vf, kb, vt, qib, kif, kib, wi = outs[7:]
    o_rw, s_fin = scan(rw)
    o_att = attend(qb, kf, vf, kb, vt, qib, kif, kib, wi)
    h, gate = _post(x, o_rw, o_att, pw, _row_tile(n, tm_post))
    y = _moe(h, gate, pw, _row_tile(n, tm_moe))
    return y, kf, vf, kif, s_fin


def kernel(x_prompt, x_sample, cache_k, cache_v, cache_kidx, state_wkv, state_shift, page_table, w_in, rw_mu, rw_w0, rw_w_up, rw_a0, rw_a_up, rw_g_up, rw_k_k, rw_k_a, rw_r_k, rw_ln_g, rw_ln_b, idx_ln_g, idx_ln_b, w_out, ln1_g, ln1_b, moe_w_grp, moe_w_exp, moe_w1, moe_w3, moe_w2, ln2_g, ln2_b):
    pw = _prepare_weights(w_in, rw_mu, rw_w0, rw_w_up, rw_a0, rw_a_up, rw_g_up, rw_k_k, rw_k_a, rw_r_k,
                          rw_ln_g, rw_ln_b, idx_ln_g, idx_ln_b, w_out, ln1_g, ln1_b,
                          moe_w_grp, moe_w_exp, moe_w1, moe_w3, moe_w2, ln2_g, ln2_b)
    bp, tp, _ = x_prompt.shape
    bs, ts, _ = x_sample.shape
    assert ts == 1
    n_pages = page_table.shape[1]
    past = n_pages * PAGE_SIZE
    assert n_pages % SCORE_PAGES == 0 and n_pages % ATTEND_PAGES == 0

    xp = x_prompt.reshape(bp * tp, D_MODEL)

    def attend_p(qb, kf, vf, kb, vt, qib, kif, kib, wi):
        return _dsa_prompt(qib, wi.T, qb, kib, kb, vt, bp, tp)

    def scan_p(rw):
        s0 = jnp.zeros((bp, N_HEADS, HEAD_DIM, HEAD_DIM), F32)
        return _scan_chunked(rw, s0, pw["rk"], pw["rwlng"], pw["rwlnb"], bp, tp)

    y_p, k_p, v_p, ki_p, wkv_p = _layer(xp, None, _rope_tables(jnp.arange(tp)),
                                        pw, attend_p, scan_p, tm_proj=512, tm_post=512, tm_moe=1024)

    xs = x_sample.reshape(bs, D_MODEL)

    kidx_t = jnp.transpose(cache_kidx, (0, 2, 1))
    k_t = jnp.transpose(cache_k, (0, 2, 3, 1))
    v_t = jnp.transpose(cache_v, (0, 2, 3, 1))

    def attend_s(qb, kf, vf, kb, vt, qib, kif, kib, wi):
        scores, score_new = _sample_scores(page_table, qib, wi, kif.astype(cache_kidx.dtype), kidx_t, bs, n_pages)
        sel = _sample_select(scores.reshape(bs, past), score_new.reshape(bs, LANES), past)
        return _sample_attend(page_table, qb, sel, kf.astype(cache_k.dtype), vf.astype(cache_v.dtype),
                              k_t, v_t, bs, n_pages)

    def scan_s(rw):
        return _scan_step(rw, state_wkv.astype(F32), pw["rk"], pw["rwlng"], pw["rwlnb"], bs)

    y_s, k_s, v_s, ki_s, wkv_s = _layer(xs, state_shift.astype(x_sample.dtype),
                                        _rope_tables(jnp.full((bs,), past, I32)),
                                        pw, attend_s, scan_s, tm_proj=256, tm_post=512, tm_moe=1024)

    heads = lambda z, b, t: z.reshape(b, t, N_HEADS, HEAD_DIM)
    heads_t = lambda z: z.reshape(bp, N_HEADS, HEAD_DIM, tp).transpose(0, 3, 1, 2)
    return (y_p.reshape(bp, tp, D_MODEL), y_s.reshape(bs, ts, D_MODEL),
            heads_t(k_p).astype(cache_k.dtype), heads(v_p, bp, tp).astype(cache_v.dtype),
            ki_p.transpose(0, 2, 1).astype(cache_kidx.dtype),
            wkv_p.astype(state_wkv.dtype), x_prompt[:, -1].astype(state_shift.dtype),
            heads(k_s, bs, ts).astype(cache_k.dtype), heads(v_s, bs, ts).astype(cache_v.dtype),
            ki_s.reshape(bs, ts, IDX_DIM).astype(cache_kidx.dtype),
            wkv_s.astype(state_wkv.dtype), x_sample[:, -1].astype(state_shift.dtype))
```

```python
import functools
import math

import jax
import jax.numpy as jnp
from jax import lax
from jax.experimental import pallas as pl
from jax.experimental.pallas import tpu as pltpu

F32 = jnp.float32
BF16 = jnp.bfloat16
I32 = jnp.int32

D_MODEL = 1024
HEAD_DIM = 64
D_RWKV = 512
D_ATT = 512
N_HEADS = 8
LORA_W = 64
LORA_A = 64
LORA_G = 128
N_IDX_HEADS = 8
IDX_DIM = 64
TOPK_MAX = 256
ROPE_THETA = 500000.0
PAGE_SIZE = 128
N_GROUPS = 4
EXPERTS_PER_GROUP = 8
N_EXPERTS = 32
D_EXPERT = 256
LN_EPS = 1e-5
GN_EPS = HEAD_DIM * 1e-5
DECAY_SCALE = math.exp(-0.5)
ALPHA = 2.0 ** 0.25
NEG = -1e30
RW_COLS = 3 * D_RWKV + LORA_W + LORA_A + LORA_G
WI_SCALE = N_IDX_HEADS ** -0.5 * IDX_DIM ** -0.5
QK_SCALE = HEAD_DIM ** -0.5 * math.log2(math.e)
INT_MIN = -(2 ** 31)
INT_MAX = 2 ** 31 - 1

LANES = 128
SUBLANES = 8
VMEM_LIMIT = 56 * 1024 * 1024
SCAN_CHUNK = 64
SCAN_STEP_CHUNKS = 4
SOLVE_BLOCK = 16
EXPERT_CHUNK = 8
DSA_QUERIES = 512
DSA_KEYS = 512
ACC_ROWS = HEAD_DIM + 16
SCORE_PAGES = 16
ATTEND_PAGES = 8


def _dot(a, b):
    return jnp.dot(a, b, preferred_element_type=F32)


def _dot_nt(a, b):
    return lax.dot_general(a, b, (((1,), (1,)), ((), ())), preferred_element_type=F32)


def _split(x):
    hi = x.astype(BF16)
    return hi, (x - hi.astype(F32)).astype(BF16)


def _dot3(a, b, nt=False):
    f = _dot_nt if nt else _dot
    ah, al = _split(a)
    bh, bl = _split(b)
    return f(ah, bh) + (f(ah, bl) + f(al, bh))


def _dot1(a, b, nt=False):
    return (_dot_nt if nt else _dot)(a.astype(BF16), b.astype(BF16))


def _params(*sem):
    return pltpu.CompilerParams(dimension_semantics=sem, vmem_limit_bytes=VMEM_LIMIT)


def _full(shape):
    nd = len(shape)
    return pl.BlockSpec(shape, lambda *_: (0,) * nd, pipeline_mode=pl.Buffered(1))


def _rope(x, cc, s1, s2):
    outs = []
    for c in range(x.shape[1] // LANES):
        xc = x[:, c * LANES:(c + 1) * LANES]
        outs.append(xc * cc + pltpu.roll(xc, LANES - 8, 1) * s1 + pltpu.roll(xc, 8, 1) * s2)
    return outs[0] if len(outs) == 1 else jnp.concatenate(outs, axis=1)


def _proj_body(x_ref, *refs, tiles_per_seq):
    if tiles_per_seq:
        (wrw_ref, watt_ref, wtail_ref, mu_ref, w0_ref, wup_ref, a0_ref, aup_ref, gup_ref, kk_ref, ka_ref,
         lng_ref, lnb_ref, cc_ref, s1_ref, s2_ref, r_o, lw_o, k_o, v_o, kkr_o, a_o, g_o, qb_o, kf_o, vf_o, kb_o,
         vt_o, qib_o, kif_o, kib_o, wi_o, carry_scr) = refs
    else:
        (xs_ref, wrw_ref, watt_ref, wtail_ref, mu_ref, w0_ref, wup_ref, a0_ref, aup_ref, gup_ref, kk_ref, ka_ref,
         lng_ref, lnb_ref, cc_ref, s1_ref, s2_ref, r_o, lw_o, k_o, v_o, kkr_o, a_o, g_o, qb_o, kf_o, vf_o, kb_o,
         vt_o, qib_o, kif_o, kib_o, wi_o) = refs
        xsb = xs_ref[...].astype(BF16)
    xb = x_ref[...].astype(BF16)
    tm = x_ref.shape[0]
    if tiles_per_seq:
        first_row = lax.broadcasted_iota(I32, (tm, 1), 0) == 0

        @pl.when(pl.program_id(0) % tiles_per_seq == 0)
        def _():
            carry_scr[...] = jnp.zeros_like(carry_scr)

    def lerp(c0, c1):
        w = wrw_ref[:, c0:c1]
        cur = _dot(xb, w)
        if tiles_per_seq:
            prev = jnp.where(first_row, carry_scr[0:1, c0:c1], pltpu.roll(cur, 1, 0))
            carry_scr[0:1, c0:c1] = cur[tm - 1:tm, :]
        else:
            prev = _dot(xsb, w)
        return cur + mu_ref[:, c0:c1] * (prev - cur)

    r_o[...] = lerp(0, D_RWKV)
    k = lerp(D_RWKV, 2 * D_RWKV)
    v_o[...] = lerp(2 * D_RWKV, 3 * D_RWKV)
    lora = lerp(3 * D_RWKV, 3 * D_RWKV + LORA_W + LORA_A)
    dg = lerp(3 * D_RWKV + LORA_W + LORA_A, RW_COLS)
    lw_o[...] = -DECAY_SCALE * jax.nn.sigmoid(w0_ref[...] + _dot(jnp.tanh(lora).astype(BF16), wup_ref[...]))
    a = jax.nn.sigmoid(a0_ref[...] + _dot(lora.astype(BF16), aup_ref[...]))
    a_o[...] = a
    g_o[...] = _dot(jax.nn.sigmoid(dg).astype(BF16), gup_ref[...])
    kkr_o[...] = k * kk_ref[...]
    k_o[...] = k * (1.0 + (a - 1.0) * ka_ref[...])

    cc, s1, s2 = cc_ref[...], s1_ref[...], s2_ref[...]
    q = _rope(_dot(xb, watt_ref[:, 0:D_ATT]), cc, s1, s2)
    qb_o[...] = (q * QK_SCALE).astype(BF16)
    ka = _rope(_dot(xb, watt_ref[:, D_ATT:2 * D_ATT]), cc, s1, s2)
    kb_o[...] = ka.astype(BF16)
    if tiles_per_seq:
        kf_o[0] = ka.T
    else:
        kf_o[...] = ka
    va = _dot(xb, watt_ref[:, 2 * D_ATT:3 * D_ATT])
    vf_o[...] = va
    vt_o[...] = va.T.astype(BF16)
    qi = _rope(_dot(xb, watt_ref[:, 3 * D_ATT:4 * D_ATT]), cc, s1, s2)
    qib_o[...] = qi.astype(BF16)

    tail = _dot(xb, wtail_ref[...])
    t0 = tail[:, :LANES]
    in_ki = lax.broadcasted_iota(I32, (1, LANES), 1) < IDX_DIM
    mean = jnp.sum(jnp.where(in_ki, t0, 0.0), axis=1, keepdims=True) * (1.0 / IDX_DIM)
    d = jnp.where(in_ki, t0 - mean, 0.0)
    var = jnp.sum(d * d, axis=1, keepdims=True) * (1.0 / IDX_DIM)
    ki_wide = _rope(d * lax.rsqrt(var + LN_EPS) * lng_ref[...] + lnb_ref[...], cc, s1, s2)
    ki = ki_wide[:, :IDX_DIM]
    if tiles_per_seq:
        kif_o[0] = ki_wide.T[:IDX_DIM, :]
    else:
        kif_o[...] = ki
    kib_o[...] = ki.astype(BF16)
    wi_o[...] = tail[:, LANES:LANES + N_IDX_HEADS] * WI_SCALE


def _rope_tables(pos):
    half = HEAD_DIM // 8
    inv = ROPE_THETA ** (-jnp.arange(half, dtype=F32) / half)
    ang = pos.astype(F32)[:, None] * inv[None, :]
    cos, sin = jnp.cos(ang), jnp.sin(ang)
    n = pos.shape[0]
    one = jnp.ones((n, HEAD_DIM - 2 * half), F32)
    zero = jnp.zeros((n, HEAD_DIM - 2 * half), F32)
    zh = jnp.zeros((n, half), F32)
    cc = jnp.concatenate([cos, cos, one], 1)
    s1 = jnp.concatenate([-sin, zh, zero], 1)
    s2 = jnp.concatenate([zh, sin, zero], 1)
    return tuple(jnp.tile(t, (1, LANES // HEAD_DIM)) for t in (cc, s1, s2))


def _proj(x, xs, tables, pw, tm):
    n = x.shape[0]
    nt = tables[0].shape[0] // tm
    tiles_per_seq = nt if xs is None else 0
    row = lambda w: pl.BlockSpec((tm, w), lambda i: (i, 0))
    tab = pl.BlockSpec((tm, LANES), lambda i: (i % nt, 0))
    weights = (pw["wrw"], pw["watt"], pw["wtail"], pw["mu"], pw["w0"], pw["wup"], pw["a0"], pw["aup"],
               pw["gup"], pw["kk"], pw["ka"], pw["lng"], pw["lnb"])
    f = lambda w, dt=F32: jax.ShapeDtypeStruct((n, w), dt)
    if tiles_per_seq:
        cache_shape = lambda w: jax.ShapeDtypeStruct((n // (nt * tm), w, nt * tm), F32)
        cache_spec = lambda w: pl.BlockSpec((1, w, tm), lambda i: (i // nt, 0, i % nt))
    else:
        cache_shape, cache_spec = f, row
    out_shape = ([f(D_RWKV)] * 7 + [f(D_ATT, BF16), cache_shape(D_ATT), f(D_ATT), f(D_ATT, BF16),
                                    jax.ShapeDtypeStruct((D_ATT, n), BF16),
                                    f(D_ATT, BF16), cache_shape(IDX_DIM), f(IDX_DIM, BF16), f(N_IDX_HEADS)])
    out_specs = ([row(D_RWKV)] * 7 + [row(D_ATT), cache_spec(D_ATT), row(D_ATT), row(D_ATT),
                                      pl.BlockSpec((D_ATT, tm), lambda i: (0, i)),
                                      row(D_ATT), cache_spec(IDX_DIM), row(IDX_DIM), row(N_IDX_HEADS)])
    rows_in = (x,) if xs is None else (x, xs)
    return pl.pallas_call(
        functools.partial(_proj_body, tiles_per_seq=tiles_per_seq),
        grid=(n // tm,),
        in_specs=[row(D_MODEL)] * len(rows_in) + [_full(w.shape) for w in weights] + [tab] * 3,
        out_specs=out_specs,
        out_shape=out_shape,
        scratch_shapes=[pltpu.VMEM((SUBLANES, RW_COLS), F32)] if xs is None else [],
        compiler_params=_params("arbitrary" if xs is None else "parallel"),
        name="proj",
    )(*rows_in, *weights, *tables)


def _head_epilogue(o, r, k, v, g, rk, lng, lnb):
    mu = jnp.mean(o, axis=-1, keepdims=True)
    d = o - mu
    var = jnp.mean(d * d, axis=-1, keepdims=True)
    on = d * lax.rsqrt(var + GN_EPS) * lng + lnb
    bonus = jnp.sum(r * k * rk, axis=-1, keepdims=True) * v
    return (on + bonus) * g


def _scan_chunk_body(r_ref, lw_ref, k_ref, v_ref, kk_ref, a_ref, g_ref, rk_ref, lng_ref, lnb_ref, s0_ref,
                     o_ref, sf_ref, s_scr):
    c = pl.program_id(1)

    @pl.when(c == 0)
    def _():
        s_scr[...] = s0_ref[0]

    C = SCAN_CHUNK
    heads = range(N_HEADS)
    row = lax.broadcasted_iota(I32, (C, 2 * C), 0)
    col = lax.broadcasted_iota(I32, (C, 2 * C), 1) % C
    strict2 = col < row
    incl2 = col <= row
    same_blk = (col // SOLVE_BLOCK) == (row // SOLVE_BLOCK)
    left = lax.broadcasted_iota(I32, (C, 2 * C), 1) < C
    tri = incl2[:, :C].astype(BF16)

    sls = [slice(h * HEAD_DIM, (h + 1) * HEAD_DIM) for h in heads]

    def chunk(ci, carry):
        rows = pl.ds(pl.multiple_of(ci * C, C), C)
        lw_all = lw_ref[rows, :]
        l_hi = lw_all.astype(BF16)
        l_r1 = lw_all - l_hi.astype(F32)
        l_mid = l_r1.astype(BF16)
        l_lo = (l_r1 - l_mid.astype(F32)).astype(BF16)
        cum_all = _dot(tri, l_hi) + (_dot(tri, l_mid) + _dot(tri, l_lo))

        r = [r_ref[rows, sl] for sl in sls]
        k = [k_ref[rows, sl] for sl in sls]
        v = [v_ref[rows, sl] for sl in sls]
        lhs1, rhs1, bk_end, g_end = [], [], [], []
        for h in heads:
            sl = sls[h]
            lw, kkr, a = lw_all[:, sl], kk_ref[rows, sl], a_ref[rows, sl]
            cum = cum_all[:, sl]
            cum_end = cum[C - 1:C, :]
            kk = kkr * lax.rsqrt(jnp.maximum(jnp.sum(kkr * kkr, axis=-1, keepdims=True), 1e-24))
            beta = kk * a
            e_neg = jnp.exp(-cum)
            e_end = jnp.exp(cum_end - cum)
            lhs1.append(jnp.concatenate([-kk * jnp.exp(cum - lw), r[h] * jnp.exp(cum)], axis=0))
            rhs1.append(jnp.concatenate([beta * e_neg, k[h] * e_neg, s_scr[h]], axis=0))
            bk_end.append(jnp.concatenate([beta * e_end, k[h] * e_end], axis=0))
            g_end.append(jnp.exp(cum_end))

        r1 = [_dot1(lhs1[h], rhs1[h], nt=True) for h in heads]
        vv = [jnp.concatenate([v[h], v[h]], axis=0) for h in heads]
        y = [r1[h][:C, 2 * C:] + _dot1(jnp.where(jnp.logical_and(strict2, ~left), r1[h][:C, :2 * C], 0.0), vv[h])
             for h in heads]
        n_pair = [r1[h][:C, :2 * C] for h in heads]
        p = [jnp.where(jnp.logical_and(strict2, same_blk), n_pair[h], 0.0)[:, :C] for h in heads]
        x = [jnp.where(left, jnp.where(jnp.logical_and(strict2, ~same_blk), n_pair[h], 0.0),
                       jnp.concatenate([y[h], y[h]], axis=1)) for h in heads]
        n_steps = SOLVE_BLOCK.bit_length() - 1
        for i in range(n_steps):
            if i + 1 < n_steps:
                rr = [_dot3(p[h], jnp.concatenate([x[h], p[h]], axis=1)) for h in heads]
                x = [x[h] + rr[h][:, :2 * C] for h in heads]
                p = [rr[h][:, 2 * C:] for h in heads]
            else:
                x = [x[h] + _dot3(p[h], x[h]) for h in heads]
        rr = [_dot3(x[h][:, :C], x[h]) for h in heads]
        x = [jnp.where(left, rr[h], x[h] + rr[h]) for h in heads]
        rr = [_dot3(x[h][:, :C], x[h]) for h in heads]
        u = [(x[h] + rr[h])[:, C:] for h in heads]

        uv = [jnp.concatenate([u[h], v[h]], axis=0) for h in heads]
        o = [r1[h][C:, 2 * C:] + _dot1(jnp.where(incl2, r1[h][C:, :2 * C], 0.0), uv[h]) for h in heads]
        for h in heads:
            s_new = rhs1[h][2 * C:, :] * g_end[h] + _dot3(uv[h].T, bk_end[h])
            s_scr[h] = s_new
            sl = sls[h]
            o_ref[rows, sl] = _head_epilogue(o[h], r[h], k[h], v[h], g_ref[rows, sl], rk_ref[:, sl],
                                             lng_ref[:, sl], lnb_ref[:, sl])
        return carry

    lax.fori_loop(0, r_ref.shape[0] // C, chunk, 0)

    @pl.when(c == pl.num_programs(1) - 1)
    def _():
        sf_ref[0] = s_scr[...]


def _scan_chunked(rw, s0, rk, lng, lnb, n_seq, t):
    rows = min(t, SCAN_CHUNK * SCAN_STEP_CHUNKS)
    nc = t // rows
    row = pl.BlockSpec((rows, D_RWKV), lambda b, c: (b * nc + c, 0))
    st = pl.BlockSpec((1, N_HEADS, HEAD_DIM, HEAD_DIM), lambda b, c: (b, 0, 0, 0))
    par = _full((1, D_RWKV))
    return pl.pallas_call(
        _scan_chunk_body,
        grid=(n_seq, nc),
        in_specs=[row] * 7 + [par] * 3 + [st],
        out_specs=[row, st],
        out_shape=[jax.ShapeDtypeStruct((n_seq * t, D_RWKV), F32),
                   jax.ShapeDtypeStruct((n_seq, N_HEADS, HEAD_DIM, HEAD_DIM), F32)],
        scratch_shapes=[pltpu.VMEM((N_HEADS, HEAD_DIM, HEAD_DIM), F32)],
        compiler_params=_params("parallel", "arbitrary"),
        name="scan_chunked",
    )(*rw, rk, lng, lnb, s0)


def _scan_step_body(r_ref, lw_ref, k_ref, v_ref, kk_ref, a_ref, g_ref, rk_ref, lng_ref, lnb_ref, s0_ref,
                    o_ref, sf_ref):
    eye = (lax.broadcasted_iota(I32, (HEAD_DIM, HEAD_DIM), 0)
           == lax.broadcasted_iota(I32, (HEAD_DIM, HEAD_DIM), 1)).astype(F32)
    outs = []
    for h in range(N_HEADS):
        sl = slice(h * HEAD_DIM, (h + 1) * HEAD_DIM)
        r, lw, k, v = r_ref[0, :, sl], lw_ref[0, :, sl], k_ref[0, :, sl], v_ref[0, :, sl]
        kkr, a = kk_ref[0, :, sl], a_ref[0, :, sl]
        kk = kkr * lax.rsqrt(jnp.maximum(jnp.sum(kkr * kkr, axis=-1, keepdims=True), 1e-24))
        s = s0_ref[0, h]
        sa = jnp.sum(s * kk, axis=1, keepdims=True)
        v_col = jnp.sum(eye * v, axis=1, keepdims=True)
        s = s * jnp.exp(lw) - sa * (kk * a) + v_col * k
        sf_ref[0, h] = s
        o_col = jnp.sum(s * r, axis=1, keepdims=True)
        o = jnp.sum(eye * o_col, axis=0, keepdims=True)
        outs.append(_head_epilogue(o, r, k, v, g_ref[0, :, sl], rk_ref[:, sl], lng_ref[:, sl], lnb_ref[:, sl]))
    o_ref[0] = jnp.concatenate(outs, axis=1)


def _scan_step(rw, s0, rk, lng, lnb, n_seq):
    row = pl.BlockSpec((1, 1, D_RWKV), lambda b: (b, 0, 0))
    st = pl.BlockSpec((1, N_HEADS, HEAD_DIM, HEAD_DIM), lambda b: (b, 0, 0, 0))
    par = _full((1, D_RWKV))
    rw3 = [z.reshape(n_seq, 1, D_RWKV) for z in rw]
    o, sf = pl.pallas_call(
        _scan_step_body,
        grid=(n_seq,),
        in_specs=[row] * 7 + [par] * 3 + [st],
        out_specs=[row, st],
        out_shape=[jax.ShapeDtypeStruct((n_seq, 1, D_RWKV), F32),
                   jax.ShapeDtypeStruct((n_seq, N_HEADS, HEAD_DIM, HEAD_DIM), F32)],
        compiler_params=_params("parallel"),
        name="scan_step",
    )(*rw3, rk, lng, lnb, s0)
    return o.reshape(n_seq, D_RWKV), sf


def _order_key(score):
    bits = pltpu.bitcast(score, I32)
    return bits ^ ((bits >> 31) & INT_MAX)


def _select_threshold(count_ge, count_tie_le, n_valid, ktop, idx_bits):
    shape = n_valid.shape
    lo = jnp.full(shape, INT_MIN + 1, I32)
    hi = jnp.where(n_valid <= ktop, lo + 1, INT_MAX)

    def cond(st):
        it, lo, hi, _ = st
        return jnp.logical_and(it < 33, jnp.max(jnp.where(lo + 1 != hi, 1.0, 0.0)) > 0.0)

    def body(st):
        it, lo, hi, c_lo = st
        mid = (lo & hi) + ((lo ^ hi) >> 1)
        cnt = count_ge(mid)
        live = lo + 1 != hi
        up = jnp.logical_and(live, cnt >= ktop)
        down = jnp.logical_and(live, cnt < ktop)
        hi2 = jnp.where(jnp.logical_and(up, cnt == ktop), mid + 1, jnp.where(down, mid, hi))
        return it + 1, jnp.where(up, mid, lo), hi2, jnp.where(up, cnt, c_lo)

    _, thr, _, c_thr = lax.while_loop(cond, body, (jnp.int32(0), lo, hi, n_valid.astype(F32)))
    excess = jnp.logical_and(c_thr > ktop, n_valid > ktop)
    has_tie = jnp.max(jnp.where(excess, 1.0, 0.0)) > 0.0
    all_idx = jnp.full(shape, (1 << idx_bits) - 1, I32)

    def tie_search():
        need = ktop - count_ge(thr + 1)

        def tbody(_, st):
            lo_j, hi_j = st
            mid = (lo_j + hi_j) >> 1
            ok = count_tie_le(thr, mid) >= need
            return jnp.where(ok, lo_j, mid + 1), jnp.where(ok, mid, hi_j)

        lo_j, _ = lax.fori_loop(0, idx_bits, tbody, (jnp.zeros(shape, I32), all_idx))
        return jnp.where(excess, lo_j, all_idx)

    tie_end = lax.cond(has_tie, tie_search, lambda: all_idx)
    return thr, tie_end, has_tie


def _dsa_prompt_body(qi_ref, wit_ref, q_ref, ki_ref, k_ref, vt_ref, o_ref, key_scr, m_scr, bias_scr, s_scr, acc_scr,
                     *, ktop, tq, tk, idx_bits):
    i = pl.program_id(1)
    n_kt = (i + 1) * (tq // tk)
    qpos = i * tq + lax.broadcasted_iota(I32, (1, tq), 1)
    key_off = lax.broadcasted_iota(I32, (tk, 1), 0)
    tile = lambda kt: pl.ds(pl.multiple_of(kt * tk, tk), tk)
    qi = qi_ref[...]
    qi_h = [qi[:, h * IDX_DIM:(h + 1) * IDX_DIM] for h in range(N_IDX_HEADS)]
    wit = wit_ref[...]

    def score_tile(kt, carry):
        kis = ki_ref[tile(kt), :]
        acc = jnp.zeros((tk, tq), F32)
        for h in range(N_IDX_HEADS):
            acc = acc + jnp.maximum(_dot_nt(kis, qi_h[h]), 0.0) * wit[h:h + 1, :]
        key_scr[tile(kt), :] = jnp.where(kt * tk + key_off <= qpos, _order_key(acc), INT_MIN)
        return carry

    lax.fori_loop(0, n_kt, score_tile, 0)

    def count(pred):
        def body(kt, acc):
            hit = jnp.where(pred(key_scr[tile(kt), :], kt * tk + key_off), 1.0, 0.0)
            return acc + jnp.sum(hit.reshape(tk // SUBLANES, SUBLANES, tq), axis=0)
        acc = lax.fori_loop(0, n_kt, body, jnp.zeros((SUBLANES, tq), F32))
        return jnp.sum(acc, axis=0, keepdims=True)

    thr, tie_end, has_tie = _select_threshold(
        lambda t: count(lambda keys, pos: keys >= t),
        lambda t, j: count(lambda keys, pos: jnp.logical_and(keys == t, pos <= j)),
        qpos + 1, ktop, idx_bits)

    @pl.when(has_tie)
    def _():
        def demote(kt, carry):
            keys = key_scr[tile(kt), :]
            late = jnp.logical_and(keys == thr, kt * tk + key_off > tie_end)
            key_scr[tile(kt), :] = jnp.where(late, thr - 1, keys)
            return carry
        lax.fori_loop(0, n_kt, demote, 0)

    q = q_ref[...]
    lane = lax.broadcasted_iota(I32, (1, LANES), 1)
    q_pad = []
    for h in range(N_HEADS):
        own = (lane < HEAD_DIM) if h % 2 == 0 else (lane >= HEAD_DIM)
        q_pad.append(jnp.where(own, q[:, (h // 2) * LANES:(h // 2 + 1) * LANES].astype(F32), 0.0).astype(BF16))
    m_scr[...] = jnp.full(m_scr.shape, NEG, F32)
    acc_scr[...] = jnp.zeros(acc_scr.shape, F32)
    ones = jnp.ones((ACC_ROWS - HEAD_DIM, tk), BF16)

    def attend(kt, carry):
        bias_scr[...] = jnp.where(key_scr[tile(kt), :] >= thr, 0.0, NEG)
        tile_max = []
        for h in range(N_HEADS):
            s = _dot_nt(k_ref[tile(kt), (h // 2) * LANES:(h // 2 + 1) * LANES], q_pad[h]) + bias_scr[...]
            s_scr[h] = s
            tile_max.append(jnp.max(s, axis=0, keepdims=True))
        m_old = m_scr[...]
        m_new = jnp.maximum(m_old, jnp.concatenate(tile_max, axis=0))
        alpha = jnp.exp2(m_old - m_new)
        m_scr[...] = m_new
        for h in range(N_HEADS):
            p = jnp.exp2(s_scr[h] - m_new[h:h + 1, :]).astype(BF16)
            v_ext = jnp.concatenate([vt_ref[h * HEAD_DIM:(h + 1) * HEAD_DIM, tile(kt)], ones], axis=0)
            rows = slice(h * ACC_ROWS, (h + 1) * ACC_ROWS)
            acc_scr[rows, :] = alpha[h:h + 1, :] * acc_scr[rows, :] + _dot(v_ext, p)
        return carry

    lax.fori_loop(0, n_kt, attend, 0)
    out_t = jnp.concatenate(
        [acc_scr[h * ACC_ROWS:h * ACC_ROWS + HEAD_DIM, :] / acc_scr[h * ACC_ROWS + HEAD_DIM:h * ACC_ROWS + HEAD_DIM + 1, :]
         for h in range(N_HEADS)], axis=0)
    o_ref[...] = out_t.T


def _dsa_prompt(qib, wit, qb, kib, kb, vt, n_seq, t):
    tq = min(t, DSA_QUERIES)
    tk = min(t, DSA_KEYS)
    nq = t // tq
    ktop = min(TOPK_MAX, t // 4)
    qrow = lambda w: pl.BlockSpec((tq, w), lambda b, i: (b * nq + i, 0))
    seq = lambda w: pl.BlockSpec((t, w), lambda b, i: (b, 0), pipeline_mode=pl.Buffered(1))
    body = functools.partial(_dsa_prompt_body, ktop=ktop, tq=tq, tk=tk, idx_bits=max(1, (t - 1).bit_length()))
    return pl.pallas_call(
        body,
        grid=(n_seq, nq),
        in_specs=[qrow(D_ATT), pl.BlockSpec((N_IDX_HEADS, tq), lambda b, i: (0, b * nq + i)), qrow(D_ATT),
                  seq(IDX_DIM), seq(D_ATT),
                  pl.BlockSpec((D_ATT, t), lambda b, i: (0, b), pipeline_mode=pl.Buffered(1))],
        out_specs=qrow(D_ATT),
        out_shape=jax.ShapeDtypeStruct((n_seq * t, D_ATT), F32),
        scratch_shapes=[pltpu.VMEM((t, tq), I32), pltpu.VMEM((N_HEADS, tq), F32), pltpu.VMEM((tk, tq), F32),
                        pltpu.VMEM((N_HEADS, tk, tq), F32), pltpu.VMEM((N_HEADS * ACC_ROWS, tq), F32)],
        compiler_params=_params("parallel", "arbitrary"),
        name="dsa_prompt",
    )(qib, wit, qb, kib, kb, vt)


def _sample_score_body(pt_ref, qi_ref, wi_ref, kin_ref, *rest):
    page_refs, (sc_ref, scn_ref) = rest[:SCORE_PAGES], rest[SCORE_PAGES:]
    qi = qi_ref[0]
    wi_col = wi_ref[0]
    pages = jnp.concatenate([p[0] for p in page_refs], axis=1).astype(BF16)
    rel = jnp.maximum(_dot(qi, pages), 0.0) * wi_col
    sc_ref[0] = 0.0 + jnp.sum(rel, axis=0, keepdims=True)

    @pl.when(pl.program_id(1) == 0)
    def _():
        kn = kin_ref[0].astype(BF16).astype(F32)
        rel_n = jnp.maximum(jnp.sum(qi.astype(F32) * kn, axis=1, keepdims=True), 0.0) * wi_col
        scn_ref[0] = jnp.broadcast_to(0.0 + jnp.sum(rel_n, axis=0, keepdims=True), (1, LANES))


def _sample_scores(page_table, qib, wi, ki_new, kidx_t, n_seq, n_pages):
    g = SCORE_PAGES
    page = lambda s: pl.BlockSpec((1, IDX_DIM, PAGE_SIZE), lambda b, j, pt: (pt[b * n_pages + j * g + s], 0, 0))
    grid_spec = pltpu.PrefetchScalarGridSpec(
        num_scalar_prefetch=1,
        grid=(n_seq, n_pages // g),
        in_specs=[pl.BlockSpec((1, N_IDX_HEADS, IDX_DIM), lambda b, j, pt: (b, 0, 0)),
                  pl.BlockSpec((1, N_IDX_HEADS, 1), lambda b, j, pt: (b, 0, 0)),
                  pl.BlockSpec((1, 1, IDX_DIM), lambda b, j, pt: (b, 0, 0))] + [page(s) for s in range(g)],
        out_specs=[pl.BlockSpec((1, 1, g * PAGE_SIZE), lambda b, j, pt: (b, 0, j)),
                   pl.BlockSpec((1, 1, LANES), lambda b, j, pt: (b, 0, 0))])
    return pl.pallas_call(
        _sample_score_body,
        grid_spec=grid_spec,
        out_shape=[jax.ShapeDtypeStruct((n_seq, 1, n_pages * PAGE_SIZE), F32),
                   jax.ShapeDtypeStruct((n_seq, 1, LANES), F32)],
        compiler_params=_params("parallel", "arbitrary"),
        name="sample_scores",
    )(page_table.reshape(-1), qib.reshape(n_seq, N_IDX_HEADS, IDX_DIM), wi.reshape(n_seq, N_IDX_HEADS, 1),
      ki_new.reshape(n_seq, 1, IDX_DIM), *([kidx_t] * g))


def _sample_select_body(sc_ref, scn_ref, sel_ref, key_scr, *, ktop, past, idx_bits):
    rows = sc_ref.shape[0]
    n_t = past // LANES + 1
    lane_pos = lax.broadcasted_iota(I32, (1, LANES), 1)
    tile = lambda kt: pl.ds(pl.multiple_of(kt * LANES, LANES), LANES)
    key_scr[:, :past] = _order_key(sc_ref[...])
    key_scr[:, past:] = jnp.where(lane_pos == 0, _order_key(scn_ref[...]), INT_MIN)

    def count(pred):
        def body(kt, acc):
            return acc + jnp.where(pred(key_scr[:, tile(kt)], kt * LANES + lane_pos), 1.0, 0.0)
        acc = lax.fori_loop(0, n_t, body, jnp.zeros((rows, LANES), F32))
        return jnp.sum(acc, axis=1, keepdims=True)

    thr, tie_end, _ = _select_threshold(
        lambda t: count(lambda keys, pos: keys >= t),
        lambda t, j: count(lambda keys, pos: jnp.logical_and(keys == t, pos <= j)),
        jnp.full((rows, 1), past + 1, I32), ktop, idx_bits)

    def write(kt, carry):
        keys = key_scr[:, tile(kt)]
        sel = jnp.logical_or(keys > thr, jnp.logical_and(keys >= thr, kt * LANES + lane_pos <= tie_end))
        sel_ref[:, tile(kt)] = jnp.where(sel, 1.0, 0.0)
        return carry

    lax.fori_loop(0, n_t, write, 0)


def _sample_select(scores, score_new, past):
    rows = scores.shape[0]
    width = past + LANES
    ktop = min(TOPK_MAX, (past + 1) // 4)
    body = functools.partial(_sample_select_body, ktop=ktop, past=past, idx_bits=width.bit_length())
    rb = min(rows, SUBLANES)
    blk = lambda w: pl.BlockSpec((rb, w), lambda i: (i, 0))
    return pl.pallas_call(
        body,
        grid=(rows // rb,),
        in_specs=[blk(past), blk(LANES)],
        out_specs=blk(width),
        out_shape=jax.ShapeDtypeStruct((rows, width), F32),
        scratch_shapes=[pltpu.VMEM((rb, width), I32)],
        compiler_params=_params("parallel"),
        name="sample_select",
    )(scores, score_new)


def _sample_attend_body(pt_ref, q_ref, sel_ref, seln_ref, kn_ref, vn_ref, *rest):
    g = ATTEND_PAGES
    kp_refs, vp_refs, (o_ref, m_scr, l_scr, acc_scr) = rest[:g], rest[g:2 * g], rest[2 * g:]
    head_tokens = lambda refs, h: jnp.concatenate([r[0, h] for r in refs], axis=1).astype(BF16)
    j = pl.program_id(1)
    q = q_ref[0]
    row = lax.broadcasted_iota(I32, (N_HEADS, 1), 0)

    @pl.when(j == 0)
    def _():
        kn = kn_ref[0].astype(BF16).astype(F32)
        logit = jnp.sum(q.astype(F32) * kn, axis=1, keepdims=True)
        on = seln_ref[0][:, 0:1] > 0.0
        m_scr[...] = jnp.where(on, logit, NEG)
        l_scr[...] = jnp.where(on, jnp.ones_like(logit), 0.0)
        acc_scr[...] = jnp.where(on, vn_ref[0].astype(BF16).astype(F32), 0.0)

    on = sel_ref[0] > 0.0
    s = jnp.zeros((N_HEADS, g * PAGE_SIZE), F32)
    for h in range(N_HEADS):
        s = jnp.where(row == h, _dot(q, head_tokens(kp_refs, h)), s)
    s = jnp.where(on, s, NEG)
    m_old = m_scr[...]
    m_new = jnp.maximum(m_old, jnp.max(s, axis=1, keepdims=True))
    alpha = jnp.exp2(m_old - m_new)
    p = jnp.where(on, jnp.exp2(s - m_new), 0.0)
    m_scr[...] = m_new
    l_scr[...] = alpha * l_scr[...] + jnp.sum(p, axis=1, keepdims=True)
    pb = p.astype(BF16)
    acc = alpha * acc_scr[...]
    for h in range(N_HEADS):
        acc = acc + jnp.where(row == h, _dot_nt(pb, head_tokens(vp_refs, h)), 0.0)
    acc_scr[...] = acc

    @pl.when(j == pl.num_programs(1) - 1)
    def _():
        o_ref[0] = acc_scr[...] / l_scr[...]


def _sample_attend(page_table, qb, sel, k_new, v_new, k_t, v_t, n_seq, n_pages):
    g = ATTEND_PAGES
    sel3 = sel.reshape(n_seq, 1, (n_pages + 1) * PAGE_SIZE)
    head = lambda: pl.BlockSpec((1, N_HEADS, HEAD_DIM), lambda b, j, pt: (b, 0, 0))
    page = lambda s: pl.BlockSpec((1, N_HEADS, HEAD_DIM, PAGE_SIZE),
                                  lambda b, j, pt: (pt[b * n_pages + j * g + s], 0, 0, 0))
    grid_spec = pltpu.PrefetchScalarGridSpec(
        num_scalar_prefetch=1,
        grid=(n_seq, n_pages // g),
        in_specs=[head(),
                  pl.BlockSpec((1, 1, g * PAGE_SIZE), lambda b, j, pt: (b, 0, j)),
                  pl.BlockSpec((1, 1, PAGE_SIZE), lambda b, j, pt: (b, 0, n_pages)),
                  head(), head()] + [page(s) for s in range(g)] * 2,
        out_specs=head(),
        scratch_shapes=[pltpu.VMEM((N_HEADS, 1), F32), pltpu.VMEM((N_HEADS, 1), F32),
                        pltpu.VMEM((N_HEADS, HEAD_DIM), F32)])
    to_heads = lambda z: z.reshape(n_seq, N_HEADS, HEAD_DIM)
    out = pl.pallas_call(
        _sample_attend_body,
        grid_spec=grid_spec,
        out_shape=jax.ShapeDtypeStruct((n_seq, N_HEADS, HEAD_DIM), F32),
        compiler_params=_params("parallel", "arbitrary"),
        name="sample_attend",
    )(page_table.reshape(-1), to_heads(qb), sel3, sel3, to_heads(k_new), to_heads(v_new),
      *([k_t] * g), *([v_t] * g))
    return out.reshape(n_seq, D_ATT)


def _layer_norm(x, g, b):
    mu = jnp.mean(x, axis=-1, keepdims=True)
    d = x - mu
    var = jnp.mean(d * d, axis=-1, keepdims=True)
    return d * lax.rsqrt(var + LN_EPS) * g + b


def _post_body(x_ref, orw_ref, oatt_ref, wo_ref, g_ref, b_ref, wr_ref, h_o, gate_o):
    mix = _dot(orw_ref[...].astype(BF16), wo_ref[0:D_RWKV, :]) + _dot(oatt_ref[...].astype(BF16), wo_ref[D_RWKV:, :])
    h = _layer_norm(ALPHA * x_ref[...] + mix, g_ref[...], b_ref[...])
    h_o[...] = h
    logit = _dot3(h, wr_ref[...])
    rows = logit.shape[0]
    lane = lax.broadcasted_iota(I32, (rows, LANES), 1).astype(F32)
    big = float(LANES)

    def first_max(mask):
        val = jnp.max(jnp.where(mask, logit, -jnp.inf), axis=1, keepdims=True)
        idx = jnp.min(jnp.where(jnp.logical_and(mask, logit == val), lane, big), axis=1, keepdims=True)
        return val, idx

    is_grp = jnp.logical_and(lane >= N_EXPERTS, lane < N_EXPERTS + N_GROUPS)
    g_max, g_lane = first_max(is_grp)
    p_grp = 1.0 / jnp.sum(jnp.where(is_grp, jnp.exp(logit - g_max), 0.0), axis=1, keepdims=True)
    in_grp = jnp.floor(lane * (1.0 / EXPERTS_PER_GROUP)) == (g_lane - N_EXPERTS)
    v1, i1 = first_max(in_grp)
    v2, i2 = first_max(jnp.logical_and(in_grp, lane != i1))
    e2 = jnp.exp(v2 - v1)
    g1 = p_grp / (1.0 + e2)
    g2 = p_grp * e2 / (1.0 + e2)
    gate_o[...] = jnp.where(lane == i1, g1, 0.0) + jnp.where(lane == i2, g2, 0.0)


def _post(x, o_rw, o_att, pw, tm):
    n = x.shape[0]
    row = lambda w: pl.BlockSpec((tm, w), lambda i: (i, 0))
    return pl.pallas_call(
        _post_body,
        grid=(n // tm,),
        in_specs=[row(D_MODEL), row(D_RWKV), row(D_ATT), _full((D_MODEL, D_MODEL)), _full((1, D_MODEL)),
                  _full((1, D_MODEL)), _full((D_MODEL, LANES))],
        out_specs=[row(D_MODEL), row(LANES)],
        out_shape=[jax.ShapeDtypeStruct((n, D_MODEL), F32), jax.ShapeDtypeStruct((n, LANES), F32)],
        compiler_params=_params("parallel"),
        name="post",
    )(x, o_rw, o_att, pw["wout"], pw["ln1g"], pw["ln1b"], pw["wrouter"])


def _moe_body(h_ref, gate_ref, w1_ref, w3_ref, w2_ref, g_ref, b_ref, y_ref, hb_scr, acc_scr):
    e = pl.program_id(1)

    @pl.when(e == 0)
    def _():
        hb_scr[...] = h_ref[...].astype(BF16)
        acc_scr[...] = jnp.zeros_like(acc_scr)

    hb = hb_scr[...]
    gates = gate_ref[0]
    for j in range(EXPERT_CHUNK):
        h1 = _dot(hb, w1_ref[j])
        h3 = _dot(hb, w3_ref[j])
        hidden = jax.nn.silu(h1) * h3 * gates[:, j:j + 1]
        acc_scr[...] += _dot(hidden.astype(BF16), w2_ref[j])

    @pl.when(e == pl.num_programs(1) - 1)
    def _():
        y_ref[...] = _layer_norm(ALPHA * h_ref[...] + acc_scr[...], g_ref[...], b_ref[...])


def _moe(h, gate, pw, tm):
    n = h.shape[0]
    n_ec = N_EXPERTS // EXPERT_CHUNK
    gate_c = gate[:, :N_EXPERTS].reshape(n, n_ec, EXPERT_CHUNK).transpose(1, 0, 2)
    return pl.pallas_call(
        _moe_body,
        grid=(n // tm, n_ec),
        in_specs=[pl.BlockSpec((tm, D_MODEL), lambda i, e: (i, 0)),
                  pl.BlockSpec((1, tm, EXPERT_CHUNK), lambda i, e: (e, i, 0)),
                  pl.BlockSpec((EXPERT_CHUNK, D_MODEL, D_EXPERT), lambda i, e: (e, 0, 0)),
                  pl.BlockSpec((EXPERT_CHUNK, D_MODEL, D_EXPERT), lambda i, e: (e, 0, 0)),
                  pl.BlockSpec((EXPERT_CHUNK, D_EXPERT, D_MODEL), lambda i, e: (e, 0, 0)),
                  _full((1, D_MODEL)), _full((1, D_MODEL))],
        out_specs=pl.BlockSpec((tm, D_MODEL), lambda i, e: (i, 0)),
        out_shape=jax.ShapeDtypeStruct((n, D_MODEL), F32),
        scratch_shapes=[pltpu.VMEM((tm, D_MODEL), BF16), pltpu.VMEM((tm, D_MODEL), F32)],
        compiler_params=_params("parallel", "arbitrary"),
        name="moe",
    )(h, gate_c, pw["w1"], pw["w3"], pw["w2"], pw["ln2g"], pw["ln2b"])


def _prepare_weights(w_in, rw_mu, rw_w0, rw_w_up, rw_a0, rw_a_up, rw_g_up, rw_k_k, rw_k_a, rw_r_k,
                     rw_ln_g, rw_ln_b, idx_ln_g, idx_ln_b, w_out, ln1_g, ln1_b,
                     moe_w_grp, moe_w_exp, moe_w1, moe_w3, moe_w2, ln2_g, ln2_b):
    row = lambda z: z.reshape(1, -1).astype(F32)
    c5 = RW_COLS + 4 * D_ATT
    pad_to = lambda z, w: jnp.pad(z, ((0, 0), (0, w - z.shape[1])))
    wtail = jnp.concatenate([pad_to(w_in[:, c5:c5 + IDX_DIM], LANES),
                             pad_to(w_in[:, c5 + IDX_DIM:], LANES)], axis=1)
    zeros_w = jnp.zeros((LORA_W, D_RWKV), F32)
    return {
        "wrw": w_in[:, :RW_COLS].astype(BF16),
        "watt": w_in[:, RW_COLS:c5].astype(BF16),
        "wtail": wtail.astype(BF16),
        "mu": row(rw_mu), "w0": row(rw_w0), "a0": row(rw_a0), "kk": row(rw_k_k), "ka": row(rw_k_a),
        "wup": jnp.concatenate([rw_w_up, zeros_w], 0).astype(BF16),
        "aup": jnp.concatenate([zeros_w, rw_a_up], 0).astype(BF16),
        "gup": rw_g_up.astype(BF16),
        "lng": pad_to(row(idx_ln_g), LANES), "lnb": pad_to(row(idx_ln_b), LANES),
        "rk": row(rw_r_k), "rwlng": row(rw_ln_g), "rwlnb": row(rw_ln_b),
        "wout": w_out.astype(BF16), "ln1g": row(ln1_g), "ln1b": row(ln1_b),
        "wrouter": pad_to(jnp.concatenate([moe_w_exp, moe_w_grp], 1).astype(F32), LANES),
        "w1": moe_w1.astype(BF16), "w3": moe_w3.astype(BF16), "w2": moe_w2.astype(BF16),
        "ln2g": row(ln2_g), "ln2b": row(ln2_b),
    }


def _row_tile(n, want):
    return min(n, want)


def _layer(x, x_shift, pos_tables, pw, attend, scan, *, tm_proj, tm_post, tm_moe):
    n = x.shape[0]
    outs = _proj(x, x_shift, pos_tables, pw, _row_tile(n, tm_proj))
    rw = outs[:7]
    qb, kf, vf, kb, vt, qib, kif, kib, wi = outs[7:]
    o_rw, s_fin = scan(rw)
    o_att = attend(qb, kf, vf, kb, vt, qib, kif, kib, wi)
    h, gate = _post(x, o_rw, o_att, pw, _row_tile(n, tm_post))
    y = _moe(h, gate, pw, _row_tile(n, tm_moe))
    return y, kf, vf, kif, s_fin


def kernel(x_prompt, x_sample, cache_k, cache_v, cache_kidx, state_wkv, state_shift, page_table, w_in, rw_mu, rw_w0, rw_w_up, rw_a0, rw_a_up, rw_g_up, rw_k_k, rw_k_a, rw_r_k, rw_ln_g, rw_ln_b, idx_ln_g, idx_ln_b, w_out, ln1_g, ln1_b, moe_w_grp, moe_w_exp, moe_w1, moe_w3, moe_w2, ln2_g, ln2_b):
    pw = _prepare_weights(w_in, rw_mu, rw_w0, rw_w_up, rw_a0, rw_a_up, rw_g_up, rw_k_k, rw_k_a, rw_r_k,
                          rw_ln_g, rw_ln_b, idx_ln_g, idx_ln_b, w_out, ln1_g, ln1_b,
                          moe_w_grp, moe_w_exp, moe_w1, moe_w3, moe_w2, ln2_g, ln2_b)
    bp, tp, _ = x_prompt.shape
    bs, ts, _ = x_sample.shape
    assert ts == 1
    n_pages = page_table.shape[1]
    past = n_pages * PAGE_SIZE
    assert n_pages % SCORE_PAGES == 0 and n_pages % ATTEND_PAGES == 0

    xp = x_prompt.reshape(bp * tp, D_MODEL)

    def attend_p(qb, kf, vf, kb, vt, qib, kif, kib, wi):
        return _dsa_prompt(qib, wi.T, qb, kib, kb, vt, bp, tp)

    def scan_p(rw):
        s0 = jnp.zeros((bp, N_HEADS, HEAD_DIM, HEAD_DIM), F32)
        return _scan_chunked(rw, s0, pw["rk"], pw["rwlng"], pw["rwlnb"], bp, tp)

    y_p, k_p, v_p, ki_p, wkv_p = _layer(xp, None, _rope_tables(jnp.arange(tp)),
                                        pw, attend_p, scan_p, tm_proj=512, tm_post=512, tm_moe=1024)

    xs = x_sample.reshape(bs, D_MODEL)

    kidx_t = jnp.transpose(cache_kidx, (0, 2, 1))
    k_t = jnp.transpose(cache_k, (0, 2, 3, 1))
    v_t = jnp.transpose(cache_v, (0, 2, 3, 1))

    def attend_s(qb, kf, vf, kb, vt, qib, kif, kib, wi):
        scores, score_new = _sample_scores(page_table, qib, wi, kif.astype(cache_kidx.dtype), kidx_t, bs, n_pages)
        sel = _sample_select(scores.reshape(bs, past), score_new.reshape(bs, LANES), past)
        return _sample_attend(page_table, qb, sel, kf.astype(cache_k.dtype), vf.astype(cache_v.dtype),
                              k_t, v_t, bs, n_pages)

    def scan_s(rw):
        return _scan_step(rw, state_wkv.astype(F32), pw["rk"], pw["rwlng"], pw["rwlnb"], bs)

    y_s, k_s, v_s, ki_s, wkv_s = _layer(xs, state_shift.astype(x_sample.dtype),
                                        _rope_tables(jnp.full((bs,), past, I32)),
                                        pw, attend_s, scan_s, tm_proj=256, tm_post=512, tm_moe=1024)

    heads = lambda z, b, t: z.reshape(b, t, N_HEADS, HEAD_DIM)
    heads_t = lambda z: z.reshape(bp, N_HEADS, HEAD_DIM, tp).transpose(0, 3, 1, 2)
    return (y_p.reshape(bp, tp, D_MODEL), y_s.reshape(bs, ts, D_MODEL),
            heads_t(k_p).astype(cache_k.dtype), heads(v_p, bp, tp).astype(cache_v.dtype),
            ki_p.transpose(0, 2, 1).astype(cache_kidx.dtype),
            wkv_p.astype(state_wkv.dtype), x_prompt[:, -1].astype(state_shift.dtype),
            heads(k_s, bs, ts).astype(cache_k.dtype), heads(v_s, bs, ts).astype(cache_v.dtype),
            ki_s.reshape(bs, ts, IDX_DIM).astype(cache_kidx.dtype),
            wkv_s.astype(state_wkv.dtype), x_sample[:, -1].astype(state_shift.dtype))
```

```python
import functools
import math

import jax
import jax.numpy as jnp
from jax import lax
from jax.experimental import pallas as pl
from jax.experimental.pallas import tpu as pltpu

F32 = jnp.float32
BF16 = jnp.bfloat16
I32 = jnp.int32

D_MODEL = 1024
HEAD_DIM = 64
D_RWKV = 512
D_ATT = 512
N_HEADS = 8
LORA_W = 64
LORA_A = 64
LORA_G = 128
N_IDX_HEADS = 8
IDX_DIM = 64
TOPK_MAX = 256
ROPE_THETA = 500000.0
PAGE_SIZE = 128
N_GROUPS = 4
EXPERTS_PER_GROUP = 8
N_EXPERTS = 32
D_EXPERT = 256
LN_EPS = 1e-5
GN_EPS = HEAD_DIM * 1e-5
DECAY_SCALE = math.exp(-0.5)
ALPHA = 2.0 ** 0.25
NEG = -1e30
RW_COLS = 3 * D_RWKV + LORA_W + LORA_A + LORA_G
WI_SCALE = N_IDX_HEADS ** -0.5 * IDX_DIM ** -0.5
QK_SCALE = HEAD_DIM ** -0.5 * math.log2(math.e)
INT_MIN = -(2 ** 31)
INT_MAX = 2 ** 31 - 1

LANES = 128
SUBLANES = 8
VMEM_LIMIT = 56 * 1024 * 1024
SCAN_CHUNK = 64
SCAN_STEP_CHUNKS = 4
SELECT_ROWS = 32
SOLVE_BLOCK = 16
EXPERT_CHUNK = 8
DSA_QUERIES = 512
DSA_KEYS = 512
ACC_ROWS = HEAD_DIM + 16
SCORE_PAGES = 16
ATTEND_PAGES = 8


def _dot(a, b):
    return jnp.dot(a, b, preferred_element_type=F32)


def _dot_nt(a, b):
    return lax.dot_general(a, b, (((1,), (1,)), ((), ())), preferred_element_type=F32)


def _split(x):
    hi = x.astype(BF16)
    return hi, (x - hi.astype(F32)).astype(BF16)


def _dot3(a, b, nt=False):
    f = _dot_nt if nt else _dot
    ah, al = _split(a)
    bh, bl = _split(b)
    return f(ah, bh) + (f(ah, bl) + f(al, bh))


def _dot1(a, b, nt=False):
    return (_dot_nt if nt else _dot)(a.astype(BF16), b.astype(BF16))


def _params(*sem):
    return pltpu.CompilerParams(dimension_semantics=sem, vmem_limit_bytes=VMEM_LIMIT)


def _full(shape):
    nd = len(shape)
    return pl.BlockSpec(shape, lambda *_: (0,) * nd, pipeline_mode=pl.Buffered(1))


def _rope(x, cc, s1, s2):
    outs = []
    for c in range(x.shape[1] // LANES):
        xc = x[:, c * LANES:(c + 1) * LANES]
        outs.append(xc * cc + pltpu.roll(xc, LANES - 8, 1) * s1 + pltpu.roll(xc, 8, 1) * s2)
    return outs[0] if len(outs) == 1 else jnp.concatenate(outs, axis=1)


def _proj_body(x_ref, *refs, tiles_per_seq):
    if tiles_per_seq:
        (wrw_ref, watt_ref, wtail_ref, mu_ref, w0_ref, wup_ref, a0_ref, aup_ref, gup_ref, kk_ref, ka_ref,
         lng_ref, lnb_ref, cc_ref, s1_ref, s2_ref, r_o, lw_o, k_o, v_o, kkr_o, a_o, g_o, qb_o, kf_o, vf_o, kb_o,
         vt_o, qib_o, kif_o, kib_o, wi_o, carry_scr) = refs
    else:
        (xs_ref, wrw_ref, watt_ref, wtail_ref, mu_ref, w0_ref, wup_ref, a0_ref, aup_ref, gup_ref, kk_ref, ka_ref,
         lng_ref, lnb_ref, cc_ref, s1_ref, s2_ref, r_o, lw_o, k_o, v_o, kkr_o, a_o, g_o, qb_o, kf_o, vf_o, kb_o,
         vt_o, qib_o, kif_o, kib_o, wi_o) = refs
        xsb = xs_ref[...].astype(BF16)
    xb = x_ref[...].astype(BF16)
    tm = x_ref.shape[0]
    if tiles_per_seq:
        first_row = lax.broadcasted_iota(I32, (tm, 1), 0) == 0

        @pl.when(pl.program_id(0) % tiles_per_seq == 0)
        def _():
            carry_scr[...] = jnp.zeros_like(carry_scr)

    def lerp(c0, c1):
        w = wrw_ref[:, c0:c1]
        cur = _dot(xb, w)
        if tiles_per_seq:
            prev = jnp.where(first_row, carry_scr[0:1, c0:c1], pltpu.roll(cur, 1, 0))
            carry_scr[0:1, c0:c1] = cur[tm - 1:tm, :]
        else:
            prev = _dot(xsb, w)
        return cur + mu_ref[:, c0:c1] * (prev - cur)

    r_o[...] = lerp(0, D_RWKV)
    k = lerp(D_RWKV, 2 * D_RWKV)
    v_o[...] = lerp(2 * D_RWKV, 3 * D_RWKV)
    lora = lerp(3 * D_RWKV, 3 * D_RWKV + LORA_W + LORA_A)
    dg = lerp(3 * D_RWKV + LORA_W + LORA_A, RW_COLS)
    lw_o[...] = -DECAY_SCALE * jax.nn.sigmoid(w0_ref[...] + _dot(jnp.tanh(lora).astype(BF16), wup_ref[...]))
    a = jax.nn.sigmoid(a0_ref[...] + _dot(lora.astype(BF16), aup_ref[...]))
    a_o[...] = a
    g_o[...] = _dot(jax.nn.sigmoid(dg).astype(BF16), gup_ref[...])
    kkr_o[...] = k * kk_ref[...]
    k_o[...] = k * (1.0 + (a - 1.0) * ka_ref[...])

    cc, s1, s2 = cc_ref[...], s1_ref[...], s2_ref[...]
    q = _rope(_dot(xb, watt_ref[:, 0:D_ATT]), cc, s1, s2)
    qb_o[...] = (q * QK_SCALE).astype(BF16)
    ka = _rope(_dot(xb, watt_ref[:, D_ATT:2 * D_ATT]), cc, s1, s2)
    kb_o[...] = ka.astype(BF16)
    if tiles_per_seq:
        kf_o[0] = ka.T
    else:
        kf_o[...] = ka
    va = _dot(xb, watt_ref[:, 2 * D_ATT:3 * D_ATT])
    vf_o[...] = va
    vt_o[...] = va.T.astype(BF16)
    qi = _rope(_dot(xb, watt_ref[:, 3 * D_ATT:4 * D_ATT]), cc, s1, s2)
    qib_o[...] = qi.astype(BF16)

    tail = _dot(xb, wtail_ref[...])
    t0 = tail[:, :LANES]
    in_ki = lax.broadcasted_iota(I32, (1, LANES), 1) < IDX_DIM
    mean = jnp.sum(jnp.where(in_ki, t0, 0.0), axis=1, keepdims=True) * (1.0 / IDX_DIM)
    d = jnp.where(in_ki, t0 - mean, 0.0)
    var = jnp.sum(d * d, axis=1, keepdims=True) * (1.0 / IDX_DIM)
    ki_wide = _rope(d * lax.rsqrt(var + LN_EPS) * lng_ref[...] + lnb_ref[...], cc, s1, s2)
    ki = ki_wide[:, :IDX_DIM]
    if tiles_per_seq:
        kif_o[0] = ki_wide.T[:IDX_DIM, :]
    else:
        kif_o[...] = ki
    kib_o[...] = ki.astype(BF16)
    wi_o[...] = tail[:, LANES:LANES + N_IDX_HEADS] * WI_SCALE


def _rope_tables(pos):
    half = HEAD_DIM // 8
    inv = ROPE_THETA ** (-jnp.arange(half, dtype=F32) / half)
    ang = pos.astype(F32)[:, None] * inv[None, :]
    cos, sin = jnp.cos(ang), jnp.sin(ang)
    n = pos.shape[0]
    one = jnp.ones((n, HEAD_DIM - 2 * half), F32)
    zero = jnp.zeros((n, HEAD_DIM - 2 * half), F32)
    zh = jnp.zeros((n, half), F32)
    cc = jnp.concatenate([cos, cos, one], 1)
    s1 = jnp.concatenate([-sin, zh, zero], 1)
    s2 = jnp.concatenate([zh, sin, zero], 1)
    return tuple(jnp.tile(t, (1, LANES // HEAD_DIM)) for t in (cc, s1, s2))


def _proj(x, xs, tables, pw, tm):
    n = x.shape[0]
    nt = tables[0].shape[0] // tm
    tiles_per_seq = nt if xs is None else 0
    row = lambda w: pl.BlockSpec((tm, w), lambda i: (i, 0))
    tab = pl.BlockSpec((tm, LANES), lambda i: (i % nt, 0))
    weights = (pw["wrw"], pw["watt"], pw["wtail"], pw["mu"], pw["w0"], pw["wup"], pw["a0"], pw["aup"],
               pw["gup"], pw["kk"], pw["ka"], pw["lng"], pw["lnb"])
    f = lambda w, dt=F32: jax.ShapeDtypeStruct((n, w), dt)
    if tiles_per_seq:
        cache_shape = lambda w: jax.ShapeDtypeStruct((n // (nt * tm), w, nt * tm), F32)
        cache_spec = lambda w: pl.BlockSpec((1, w, tm), lambda i: (i // nt, 0, i % nt))
    else:
        cache_shape, cache_spec = f, row
    out_shape = ([f(D_RWKV)] * 7 + [f(D_ATT, BF16), cache_shape(D_ATT), f(D_ATT), f(D_ATT, BF16),
                                    jax.ShapeDtypeStruct((D_ATT, n), BF16),
                                    f(D_ATT, BF16), cache_shape(IDX_DIM), f(IDX_DIM, BF16), f(N_IDX_HEADS)])
    out_specs = ([row(D_RWKV)] * 7 + [row(D_ATT), cache_spec(D_ATT), row(D_ATT), row(D_ATT),
                                      pl.BlockSpec((D_ATT, tm), lambda i: (0, i)),
                                      row(D_ATT), cache_spec(IDX_DIM), row(IDX_DIM), row(N_IDX_HEADS)])
    rows_in = (x,) if xs is None else (x, xs)
    return pl.pallas_call(
        functools.partial(_proj_body, tiles_per_seq=tiles_per_seq),
        grid=(n // tm,),
        in_specs=[row(D_MODEL)] * len(rows_in) + [_full(w.shape) for w in weights] + [tab] * 3,
        out_specs=out_specs,
        out_shape=out_shape,
        scratch_shapes=[pltpu.VMEM((SUBLANES, RW_COLS), F32)] if xs is None else [],
        compiler_params=_params("arbitrary" if xs is None else "parallel"),
        name="proj",
    )(*rows_in, *weights, *tables)


def _head_epilogue(o, r, k, v, g, rk, lng, lnb):
    mu = jnp.mean(o, axis=-1, keepdims=True)
    d = o - mu
    var = jnp.mean(d * d, axis=-1, keepdims=True)
    on = d * lax.rsqrt(var + GN_EPS) * lng + lnb
    bonus = jnp.sum(r * k * rk, axis=-1, keepdims=True) * v
    return (on + bonus) * g


def _scan_chunk_body(r_ref, lw_ref, k_ref, v_ref, kk_ref, a_ref, g_ref, rk_ref, lng_ref, lnb_ref, s0_ref,
                     o_ref, sf_ref, s_scr):
    c = pl.program_id(1)

    @pl.when(c == 0)
    def _():
        s_scr[...] = s0_ref[0]

    C = SCAN_CHUNK
    heads = range(N_HEADS)
    row = lax.broadcasted_iota(I32, (C, 2 * C), 0)
    col = lax.broadcasted_iota(I32, (C, 2 * C), 1) % C
    strict2 = col < row
    incl2 = col <= row
    same_blk = (col // SOLVE_BLOCK) == (row // SOLVE_BLOCK)
    left = lax.broadcasted_iota(I32, (C, 2 * C), 1) < C
    tri = incl2[:, :C].astype(BF16)

    sls = [slice(h * HEAD_DIM, (h + 1) * HEAD_DIM) for h in heads]

    def chunk(ci, carry):
        rows = pl.ds(pl.multiple_of(ci * C, C), C)
        lw_all = lw_ref[rows, :]
        l_hi = lw_all.astype(BF16)
        l_r1 = lw_all - l_hi.astype(F32)
        l_mid = l_r1.astype(BF16)
        l_lo = (l_r1 - l_mid.astype(F32)).astype(BF16)
        cum_all = _dot(tri, l_hi) + (_dot(tri, l_mid) + _dot(tri, l_lo))

        r = [r_ref[rows, sl] for sl in sls]
        k = [k_ref[rows, sl] for sl in sls]
        v = [v_ref[rows, sl] for sl in sls]
        lhs1, rhs1, bk_end, g_end = [], [], [], []
        for h in heads:
            sl = sls[h]
            lw, kkr, a = lw_all[:, sl], kk_ref[rows, sl], a_ref[rows, sl]
            cum = cum_all[:, sl]
            cum_end = cum[C - 1:C, :]
            kk = kkr * lax.rsqrt(jnp.maximum(jnp.sum(kkr * kkr, axis=-1, keepdims=True), 1e-24))
            beta = kk * a
            e_neg = jnp.exp(-cum)
            e_end = jnp.exp(cum_end - cum)
            lhs1.append(jnp.concatenate([-kk * jnp.exp(cum - lw), r[h] * jnp.exp(cum)], axis=0))
            rhs1.append(jnp.concatenate([beta * e_neg, k[h] * e_neg, s_scr[h]], axis=0))
            bk_end.append(jnp.concatenate([beta * e_end, k[h] * e_end], axis=0))
            g_end.append(jnp.exp(cum_end))

        r1 = [_dot1(lhs1[h], rhs1[h], nt=True) for h in heads]
        vv = [jnp.concatenate([v[h], v[h]], axis=0) for h in heads]
        y = [r1[h][:C, 2 * C:] + _dot1(jnp.where(jnp.logical_and(strict2, ~left), r1[h][:C, :2 * C], 0.0), vv[h])
             for h in heads]
        n_pair = [r1[h][:C, :2 * C] for h in heads]
        p = [jnp.where(jnp.logical_and(strict2, same_blk), n_pair[h], 0.0)[:, :C] for h in heads]
        x = [jnp.where(left, jnp.where(jnp.logical_and(strict2, ~same_blk), n_pair[h], 0.0),
                       jnp.concatenate([y[h], y[h]], axis=1)) for h in heads]
        n_steps = SOLVE_BLOCK.bit_length() - 1
        for i in range(n_steps):
            if i + 1 < n_steps:
                rr = [_dot3(p[h], jnp.concatenate([x[h], p[h]], axis=1)) for h in heads]
                x = [x[h] + rr[h][:, :2 * C] for h in heads]
                p = [rr[h][:, 2 * C:] for h in heads]
            else:
                x = [x[h] + _dot3(p[h], x[h]) for h in heads]
        rr = [_dot3(x[h][:, :C], x[h]) for h in heads]
        x = [jnp.where(left, rr[h], x[h] + rr[h]) for h in heads]
        rr = [_dot3(x[h][:, :C], x[h]) for h in heads]
        u = [(x[h] + rr[h])[:, C:] for h in heads]

        uv = [jnp.concatenate([u[h], v[h]], axis=0) for h in heads]
        o = [r1[h][C:, 2 * C:] + _dot1(jnp.where(incl2, r1[h][C:, :2 * C], 0.0), uv[h]) for h in heads]
        for h in heads:
            s_new = rhs1[h][2 * C:, :] * g_end[h] + _dot3(uv[h].T, bk_end[h])
            s_scr[h] = s_new
            sl = sls[h]
            o_ref[rows, sl] = _head_epilogue(o[h], r[h], k[h], v[h], g_ref[rows, sl], rk_ref[:, sl],
                                             lng_ref[:, sl], lnb_ref[:, sl])
        return carry

    lax.fori_loop(0, r_ref.shape[0] // C, chunk, 0)

    @pl.when(c == pl.num_programs(1) - 1)
    def _():
        sf_ref[0] = s_scr[...]


def _scan_chunked(rw, s0, rk, lng, lnb, n_seq, t):
    rows = min(t, SCAN_CHUNK * SCAN_STEP_CHUNKS)
    nc = t // rows
    row = pl.BlockSpec((rows, D_RWKV), lambda b, c: (b * nc + c, 0))
    st = pl.BlockSpec((1, N_HEADS, HEAD_DIM, HEAD_DIM), lambda b, c: (b, 0, 0, 0))
    par = _full((1, D_RWKV))
    return pl.pallas_call(
        _scan_chunk_body,
        grid=(n_seq, nc),
        in_specs=[row] * 7 + [par] * 3 + [st],
        out_specs=[row, st],
        out_shape=[jax.ShapeDtypeStruct((n_seq * t, D_RWKV), F32),
                   jax.ShapeDtypeStruct((n_seq, N_HEADS, HEAD_DIM, HEAD_DIM), F32)],
        scratch_shapes=[pltpu.VMEM((N_HEADS, HEAD_DIM, HEAD_DIM), F32)],
        compiler_params=_params("parallel", "arbitrary"),
        name="scan_chunked",
    )(*rw, rk, lng, lnb, s0)


def _scan_step_body(r_ref, lw_ref, k_ref, v_ref, kk_ref, a_ref, g_ref, rk_ref, lng_ref, lnb_ref, s0_ref,
                    o_ref, sf_ref):
    eye = (lax.broadcasted_iota(I32, (HEAD_DIM, HEAD_DIM), 0)
           == lax.broadcasted_iota(I32, (HEAD_DIM, HEAD_DIM), 1)).astype(F32)
    outs = []
    for h in range(N_HEADS):
        sl = slice(h * HEAD_DIM, (h + 1) * HEAD_DIM)
        r, lw, k, v = r_ref[0, :, sl], lw_ref[0, :, sl], k_ref[0, :, sl], v_ref[0, :, sl]
        kkr, a = kk_ref[0, :, sl], a_ref[0, :, sl]
        kk = kkr * lax.rsqrt(jnp.maximum(jnp.sum(kkr * kkr, axis=-1, keepdims=True), 1e-24))
        s = s0_ref[0, h]
        sa = jnp.sum(s * kk, axis=1, keepdims=True)
        v_col = jnp.sum(eye * v, axis=1, keepdims=True)
        s = s * jnp.exp(lw) - sa * (kk * a) + v_col * k
        sf_ref[0, h] = s
        o_col = jnp.sum(s * r, axis=1, keepdims=True)
        o = jnp.sum(eye * o_col, axis=0, keepdims=True)
        outs.append(_head_epilogue(o, r, k, v, g_ref[0, :, sl], rk_ref[:, sl], lng_ref[:, sl], lnb_ref[:, sl]))
    o_ref[0] = jnp.concatenate(outs, axis=1)


def _scan_step(rw, s0, rk, lng, lnb, n_seq):
    row = pl.BlockSpec((1, 1, D_RWKV), lambda b: (b, 0, 0))
    st = pl.BlockSpec((1, N_HEADS, HEAD_DIM, HEAD_DIM), lambda b: (b, 0, 0, 0))
    par = _full((1, D_RWKV))
    rw3 = [z.reshape(n_seq, 1, D_RWKV) for z in rw]
    o, sf = pl.pallas_call(
        _scan_step_body,
        grid=(n_seq,),
        in_specs=[row] * 7 + [par] * 3 + [st],
        out_specs=[row, st],
        out_shape=[jax.ShapeDtypeStruct((n_seq, 1, D_RWKV), F32),
                   jax.ShapeDtypeStruct((n_seq, N_HEADS, HEAD_DIM, HEAD_DIM), F32)],
        compiler_params=_params("parallel"),
        name="scan_step",
    )(*rw3, rk, lng, lnb, s0)
    return o.reshape(n_seq, D_RWKV), sf


def _order_key(score):
    bits = pltpu.bitcast(score, I32)
    return bits ^ ((bits >> 31) & INT_MAX)


def _select_threshold(count_ge, count_tie_le, n_valid, ktop, idx_bits):
    shape = n_valid.shape
    lo = jnp.full(shape, INT_MIN + 1, I32)
    hi = jnp.where(n_valid <= ktop, lo + 1, INT_MAX)

    def cond(st):
        it, lo, hi, _ = st
        return jnp.logical_and(it < 33, jnp.max(jnp.where(lo + 1 != hi, 1.0, 0.0)) > 0.0)

    def body(st):
        it, lo, hi, c_lo = st
        mid = (lo & hi) + ((lo ^ hi) >> 1)
        cnt = count_ge(mid)
        live = lo + 1 != hi
        up = jnp.logical_and(live, cnt >= ktop)
        down = jnp.logical_and(live, cnt < ktop)
        hi2 = jnp.where(jnp.logical_and(up, cnt == ktop), mid + 1, jnp.where(down, mid, hi))
        return it + 1, jnp.where(up, mid, lo), hi2, jnp.where(up, cnt, c_lo)

    _, thr, _, c_thr = lax.while_loop(cond, body, (jnp.int32(0), lo, hi, n_valid.astype(F32)))
    excess = jnp.logical_and(c_thr > ktop, n_valid > ktop)
    has_tie = jnp.max(jnp.where(excess, 1.0, 0.0)) > 0.0
    all_idx = jnp.full(shape, (1 << idx_bits) - 1, I32)

    def tie_search():
        need = ktop - count_ge(thr + 1)

        def tbody(_, st):
            lo_j, hi_j = st
            mid = (lo_j + hi_j) >> 1
            ok = count_tie_le(thr, mid) >= need
            return jnp.where(ok, lo_j, mid + 1), jnp.where(ok, mid, hi_j)

        lo_j, _ = lax.fori_loop(0, idx_bits, tbody, (jnp.zeros(shape, I32), all_idx))
        return jnp.where(excess, lo_j, all_idx)

    tie_end = lax.cond(has_tie, tie_search, lambda: all_idx)
    return thr, tie_end, has_tie


def _dsa_prompt_body(qi_ref, wit_ref, q_ref, ki_ref, k_ref, vt_ref, o_ref, key_scr, m_scr, bias_scr, s_scr, acc_scr,
                     *, ktop, tq, tk, idx_bits):
    i = pl.program_id(1)
    n_kt = (i + 1) * (tq // tk)
    qpos = i * tq + lax.broadcasted_iota(I32, (1, tq), 1)
    key_off = lax.broadcasted_iota(I32, (tk, 1), 0)
    tile = lambda kt: pl.ds(pl.multiple_of(kt * tk, tk), tk)
    qi = qi_ref[...]
    qi_h = [qi[:, h * IDX_DIM:(h + 1) * IDX_DIM] for h in range(N_IDX_HEADS)]
    wit = wit_ref[...]

    def score_tile(kt, carry):
        kis = ki_ref[tile(kt), :]
        acc = jnp.zeros((tk, tq), F32)
        for h in range(N_IDX_HEADS):
            acc = acc + jnp.maximum(_dot_nt(kis, qi_h[h]), 0.0) * wit[h:h + 1, :]
        key_scr[tile(kt), :] = jnp.where(kt * tk + key_off <= qpos, _order_key(acc), INT_MIN)
        return carry

    lax.fori_loop(0, n_kt, score_tile, 0)

    def count(pred):
        def body(kt, acc):
            hit = jnp.where(pred(key_scr[tile(kt), :], kt * tk + key_off), 1.0, 0.0)
            return acc + jnp.sum(hit.reshape(tk // SUBLANES, SUBLANES, tq), axis=0)
        acc = lax.fori_loop(0, n_kt, body, jnp.zeros((SUBLANES, tq), F32))
        return jnp.sum(acc, axis=0, keepdims=True)

    thr, tie_end, has_tie = _select_threshold(
        lambda t: count(lambda keys, pos: keys >= t),
        lambda t, j: count(lambda keys, pos: jnp.logical_and(keys == t, pos <= j)),
        qpos + 1, ktop, idx_bits)

    @pl.when(has_tie)
    def _():
        def demote(kt, carry):
            keys = key_scr[tile(kt), :]
            late = jnp.logical_and(keys == thr, kt * tk + key_off > tie_end)
            key_scr[tile(kt), :] = jnp.where(late, thr - 1, keys)
            return carry
        lax.fori_loop(0, n_kt, demote, 0)

    q = q_ref[...]
    lane = lax.broadcasted_iota(I32, (1, LANES), 1)
    q_pad = []
    for h in range(N_HEADS):
        own = (lane < HEAD_DIM) if h % 2 == 0 else (lane >= HEAD_DIM)
        q_pad.append(jnp.where(own, q[:, (h // 2) * LANES:(h // 2 + 1) * LANES].astype(F32), 0.0).astype(BF16))
    m_scr[...] = jnp.full(m_scr.shape, NEG, F32)
    acc_scr[...] = jnp.zeros(acc_scr.shape, F32)
    ones = jnp.ones((ACC_ROWS - HEAD_DIM, tk), BF16)

    def attend(kt, carry):
        bias_scr[...] = jnp.where(key_scr[tile(kt), :] >= thr, 0.0, NEG)
        tile_max = []
        for h in range(N_HEADS):
            s = _dot_nt(k_ref[tile(kt), (h // 2) * LANES:(h // 2 + 1) * LANES], q_pad[h]) + bias_scr[...]
            s_scr[h] = s
            tile_max.append(jnp.max(s, axis=0, keepdims=True))
        m_old = m_scr[...]
        m_new = jnp.maximum(m_old, jnp.concatenate(tile_max, axis=0))
        alpha = jnp.exp2(m_old - m_new)
        m_scr[...] = m_new
        for h in range(N_HEADS):
            p = jnp.exp2(s_scr[h] - m_new[h:h + 1, :]).astype(BF16)
            v_ext = jnp.concatenate([vt_ref[h * HEAD_DIM:(h + 1) * HEAD_DIM, tile(kt)], ones], axis=0)
            rows = slice(h * ACC_ROWS, (h + 1) * ACC_ROWS)
            acc_scr[rows, :] = alpha[h:h + 1, :] * acc_scr[rows, :] + _dot(v_ext, p)
        return carry

    lax.fori_loop(0, n_kt, attend, 0)
    out_t = jnp.concatenate(
        [acc_scr[h * ACC_ROWS:h * ACC_ROWS + HEAD_DIM, :] / acc_scr[h * ACC_ROWS + HEAD_DIM:h * ACC_ROWS + HEAD_DIM + 1, :]
         for h in range(N_HEADS)], axis=0)
    o_ref[...] = out_t.T


def _dsa_prompt(qib, wit, qb, kib, kb, vt, n_seq, t):
    tq = min(t, DSA_QUERIES)
    tk = min(t, DSA_KEYS)
    nq = t // tq
    ktop = min(TOPK_MAX, t // 4)
    qrow = lambda w: pl.BlockSpec((tq, w), lambda b, i: (b * nq + i, 0))
    seq = lambda w: pl.BlockSpec((t, w), lambda b, i: (b, 0), pipeline_mode=pl.Buffered(1))
    body = functools.partial(_dsa_prompt_body, ktop=ktop, tq=tq, tk=tk, idx_bits=max(1, (t - 1).bit_length()))
    return pl.pallas_call(
        body,
        grid=(n_seq, nq),
        in_specs=[qrow(D_ATT), pl.BlockSpec((N_IDX_HEADS, tq), lambda b, i: (0, b * nq + i)), qrow(D_ATT),
                  seq(IDX_DIM), seq(D_ATT),
                  pl.BlockSpec((D_ATT, t), lambda b, i: (0, b), pipeline_mode=pl.Buffered(1))],
        out_specs=qrow(D_ATT),
        out_shape=jax.ShapeDtypeStruct((n_seq * t, D_ATT), F32),
        scratch_shapes=[pltpu.VMEM((t, tq), I32), pltpu.VMEM((N_HEADS, tq), F32), pltpu.VMEM((tk, tq), F32),
                        pltpu.VMEM((N_HEADS, tk, tq), F32), pltpu.VMEM((N_HEADS * ACC_ROWS, tq), F32)],
        compiler_params=_params("parallel", "arbitrary"),
        name="dsa_prompt",
    )(qib, wit, qb, kib, kb, vt)


def _sample_score_body(pt_ref, qi_ref, wi_ref, kin_ref, *rest):
    page_refs, (sc_ref, scn_ref) = rest[:SCORE_PAGES], rest[SCORE_PAGES:]
    qi = qi_ref[0]
    wi_col = wi_ref[0]
    pages = jnp.concatenate([p[0] for p in page_refs], axis=1).astype(BF16)
    rel = jnp.maximum(_dot(qi, pages), 0.0) * wi_col
    sc_ref[0] = 0.0 + jnp.sum(rel, axis=0, keepdims=True)

    @pl.when(pl.program_id(1) == 0)
    def _():
        kn = kin_ref[0].astype(BF16).astype(F32)
        rel_n = jnp.maximum(jnp.sum(qi.astype(F32) * kn, axis=1, keepdims=True), 0.0) * wi_col
        scn_ref[0] = jnp.broadcast_to(0.0 + jnp.sum(rel_n, axis=0, keepdims=True), (1, LANES))


def _sample_scores(page_table, qib, wi, ki_new, kidx_t, n_seq, n_pages):
    g = SCORE_PAGES
    page = lambda s: pl.BlockSpec((1, IDX_DIM, PAGE_SIZE), lambda b, j, pt: (pt[b * n_pages + j * g + s], 0, 0))
    grid_spec = pltpu.PrefetchScalarGridSpec(
        num_scalar_prefetch=1,
        grid=(n_seq, n_pages // g),
        in_specs=[pl.BlockSpec((1, N_IDX_HEADS, IDX_DIM), lambda b, j, pt: (b, 0, 0)),
                  pl.BlockSpec((1, N_IDX_HEADS, 1), lambda b, j, pt: (b, 0, 0)),
                  pl.BlockSpec((1, 1, IDX_DIM), lambda b, j, pt: (b, 0, 0))] + [page(s) for s in range(g)],
        out_specs=[pl.BlockSpec((1, 1, g * PAGE_SIZE), lambda b, j, pt: (b, 0, j)),
                   pl.BlockSpec((1, 1, LANES), lambda b, j, pt: (b, 0, 0))])
    return pl.pallas_call(
        _sample_score_body,
        grid_spec=grid_spec,
        out_shape=[jax.ShapeDtypeStruct((n_seq, 1, n_pages * PAGE_SIZE), F32),
                   jax.ShapeDtypeStruct((n_seq, 1, LANES), F32)],
        compiler_params=_params("parallel", "arbitrary"),
        name="sample_scores",
    )(page_table.reshape(-1), qib.reshape(n_seq, N_IDX_HEADS, IDX_DIM), wi.reshape(n_seq, N_IDX_HEADS, 1),
      ki_new.reshape(n_seq, 1, IDX_DIM), *([kidx_t] * g))


def _sample_select_body(sc_ref, scn_ref, sel_ref, key_scr, *, ktop, past, idx_bits):
    rows = sc_ref.shape[0]
    n_t = past // LANES + 1
    lane_pos = lax.broadcasted_iota(I32, (1, LANES), 1)
    tile = lambda kt: pl.ds(pl.multiple_of(kt * LANES, LANES), LANES)
    key_scr[:, :past] = _order_key(sc_ref[...])
    key_scr[:, past:] = jnp.where(lane_pos == 0, _order_key(scn_ref[...]), INT_MIN)

    def count(pred):
        def body(kt, acc):
            return acc + jnp.where(pred(key_scr[:, tile(kt)], kt * LANES + lane_pos), 1.0, 0.0)
        acc = lax.fori_loop(0, n_t, body, jnp.zeros((rows, LANES), F32))
        return jnp.sum(acc, axis=1, keepdims=True)

    thr, tie_end, _ = _select_threshold(
        lambda t: count(lambda keys, pos: keys >= t),
        lambda t, j: count(lambda keys, pos: jnp.logical_and(keys == t, pos <= j)),
        jnp.full((rows, 1), past + 1, I32), ktop, idx_bits)

    def write(kt, carry):
        keys = key_scr[:, tile(kt)]
        sel = jnp.logical_or(keys > thr, jnp.logical_and(keys >= thr, kt * LANES + lane_pos <= tie_end))
        sel_ref[:, tile(kt)] = jnp.where(sel, 1.0, 0.0)
        return carry

    lax.fori_loop(0, n_t, write, 0)


def _sample_select(scores, score_new, past):
    rows = scores.shape[0]
    width = past + LANES
    ktop = min(TOPK_MAX, (past + 1) // 4)
    body = functools.partial(_sample_select_body, ktop=ktop, past=past, idx_bits=width.bit_length())
    rb = min(rows, SELECT_ROWS)
    blk = lambda w: pl.BlockSpec((rb, w), lambda i: (i, 0))
    return pl.pallas_call(
        body,
        grid=(rows // rb,),
        in_specs=[blk(past), blk(LANES)],
        out_specs=blk(width),
        out_shape=jax.ShapeDtypeStruct((rows, width), F32),
        scratch_shapes=[pltpu.VMEM((rb, width), I32)],
        compiler_params=_params("parallel"),
        name="sample_select",
    )(scores, score_new)


def _sample_attend_body(pt_ref, q_ref, sel_ref, seln_ref, kn_ref, vn_ref, *rest):
    g = ATTEND_PAGES
    kp_refs, vp_refs, (o_ref, m_scr, l_scr, acc_scr) = rest[:g], rest[g:2 * g], rest[2 * g:]
    head_tokens = lambda refs, h: jnp.concatenate([r[0, h] for r in refs], axis=1).astype(BF16)
    j = pl.program_id(1)
    q = q_ref[0]
    row = lax.broadcasted_iota(I32, (N_HEADS, 1), 0)

    @pl.when(j == 0)
    def _():
        kn = kn_ref[0].astype(BF16).astype(F32)
        logit = jnp.sum(q.astype(F32) * kn, axis=1, keepdims=True)
        on = seln_ref[0][:, 0:1] > 0.0
        m_scr[...] = jnp.where(on, logit, NEG)
        l_scr[...] = jnp.where(on, jnp.ones_like(logit), 0.0)
        acc_scr[...] = jnp.where(on, vn_ref[0].astype(BF16).astype(F32), 0.0)

    on = sel_ref[0] > 0.0
    s = jnp.zeros((N_HEADS, g * PAGE_SIZE), F32)
    for h in range(N_HEADS):
        s = jnp.where(row == h, _dot(q, head_tokens(kp_refs, h)), s)
    s = jnp.where(on, s, NEG)
    m_old = m_scr[...]
    m_new = jnp.maximum(m_old, jnp.max(s, axis=1, keepdims=True))
    alpha = jnp.exp2(m_old - m_new)
    p = jnp.where(on, jnp.exp2(s - m_new), 0.0)
    m_scr[...] = m_new
    l_scr[...] = alpha * l_scr[...] + jnp.sum(p, axis=1, keepdims=True)
    pb = p.astype(BF16)
    acc = alpha * acc_scr[...]
    for h in range(N_HEADS):
        acc = acc + jnp.where(row == h, _dot_nt(pb, head_tokens(vp_refs, h)), 0.0)
    acc_scr[...] = acc

    @pl.when(j == pl.num_programs(1) - 1)
    def _():
        o_ref[0] = acc_scr[...] / l_scr[...]


def _sample_attend(page_table, qb, sel, k_new, v_new, k_t, v_t, n_seq, n_pages):
    g = ATTEND_PAGES
    sel3 = sel.reshape(n_seq, 1, (n_pages + 1) * PAGE_SIZE)
    head = lambda: pl.BlockSpec((1, N_HEADS, HEAD_DIM), lambda b, j, pt: (b, 0, 0))
    page = lambda s: pl.BlockSpec((1, N_HEADS, HEAD_DIM, PAGE_SIZE),
                                  lambda b, j, pt: (pt[b * n_pages + j * g + s], 0, 0, 0))
    grid_spec = pltpu.PrefetchScalarGridSpec(
        num_scalar_prefetch=1,
        grid=(n_seq, n_pages // g),
        in_specs=[head(),
                  pl.BlockSpec((1, 1, g * PAGE_SIZE), lambda b, j, pt: (b, 0, j)),
                  pl.BlockSpec((1, 1, PAGE_SIZE), lambda b, j, pt: (b, 0, n_pages)),
                  head(), head()] + [page(s) for s in range(g)] * 2,
        out_specs=head(),
        scratch_shapes=[pltpu.VMEM((N_HEADS, 1), F32), pltpu.VMEM((N_HEADS, 1), F32),
                        pltpu.VMEM((N_HEADS, HEAD_DIM), F32)])
    to_heads = lambda z: z.reshape(n_seq, N_HEADS, HEAD_DIM)
    out = pl.pallas_call(
        _sample_attend_body,
        grid_spec=grid_spec,
        out_shape=jax.ShapeDtypeStruct((n_seq, N_HEADS, HEAD_DIM), F32),
        compiler_params=_params("parallel", "arbitrary"),
        name="sample_attend",
    )(page_table.reshape(-1), to_heads(qb), sel3, sel3, to_heads(k_new), to_heads(v_new),
      *([k_t] * g), *([v_t] * g))
    return out.reshape(n_seq, D_ATT)


def _layer_norm(x, g, b):
    mu = jnp.mean(x, axis=-1, keepdims=True)
    d = x - mu
    var = jnp.mean(d * d, axis=-1, keepdims=True)
    return d * lax.rsqrt(var + LN_EPS) * g + b


def _post_body(x_ref, orw_ref, oatt_ref, wo_ref, g_ref, b_ref, wr_ref, h_o, gate_o):
    mix = _dot(orw_ref[...].astype(BF16), wo_ref[0:D_RWKV, :]) + _dot(oatt_ref[...].astype(BF16), wo_ref[D_RWKV:, :])
    h = _layer_norm(ALPHA * x_ref[...] + mix, g_ref[...], b_ref[...])
    h_o[...] = h
    logit = _dot3(h, wr_ref[...])
    rows = logit.shape[0]
    lane = lax.broadcasted_iota(I32, (rows, LANES), 1).astype(F32)
    big = float(LANES)

    def first_max(mask):
        val = jnp.max(jnp.where(mask, logit, -jnp.inf), axis=1, keepdims=True)
        idx = jnp.min(jnp.where(jnp.logical_and(mask, logit == val), lane, big), axis=1, keepdims=True)
        return val, idx

    is_grp = jnp.logical_and(lane >= N_EXPERTS, lane < N_EXPERTS + N_GROUPS)
    g_max, g_lane = first_max(is_grp)
    p_grp = 1.0 / jnp.sum(jnp.where(is_grp, jnp.exp(logit - g_max), 0.0), axis=1, keepdims=True)
    in_grp = jnp.floor(lane * (1.0 / EXPERTS_PER_GROUP)) == (g_lane - N_EXPERTS)
    v1, i1 = first_max(in_grp)
    v2, i2 = first_max(jnp.logical_and(in_grp, lane != i1))
    e2 = jnp.exp(v2 - v1)
    g1 = p_grp / (1.0 + e2)
    g2 = p_grp * e2 / (1.0 + e2)
    gate_o[...] = jnp.where(lane == i1, g1, 0.0) + jnp.where(lane == i2, g2, 0.0)


def _post(x, o_rw, o_att, pw, tm):
    n = x.shape[0]
    row = lambda w: pl.BlockSpec((tm, w), lambda i: (i, 0))
    return pl.pallas_call(
        _post_body,
        grid=(n // tm,),
        in_specs=[row(D_MODEL), row(D_RWKV), row(D_ATT), _full((D_MODEL, D_MODEL)), _full((1, D_MODEL)),
                  _full((1, D_MODEL)), _full((D_MODEL, LANES))],
        out_specs=[row(D_MODEL), row(LANES)],
        out_shape=[jax.ShapeDtypeStruct((n, D_MODEL), F32), jax.ShapeDtypeStruct((n, LANES), F32)],
        compiler_params=_params("parallel"),
        name="post",
    )(x, o_rw, o_att, pw["wout"], pw["ln1g"], pw["ln1b"], pw["wrouter"])


def _moe_body(h_ref, gate_ref, w1_ref, w3_ref, w2_ref, g_ref, b_ref, y_ref, hb_scr, acc_scr):
    e = pl.program_id(1)

    @pl.when(e == 0)
    def _():
        hb_scr[...] = h_ref[...].astype(BF16)
        acc_scr[...] = jnp.zeros_like(acc_scr)

    hb = hb_scr[...]
    gates = gate_ref[0]
    for j in range(EXPERT_CHUNK):
        h1 = _dot(hb, w1_ref[j])
        h3 = _dot(hb, w3_ref[j])
        hidden = jax.nn.silu(h1) * h3 * gates[:, j:j + 1]
        acc_scr[...] += _dot(hidden.astype(BF16), w2_ref[j])

    @pl.when(e == pl.num_programs(1) - 1)
    def _():
        y_ref[...] = _layer_norm(ALPHA * h_ref[...] + acc_scr[...], g_ref[...], b_ref[...])


def _moe(h, gate, pw, tm):
    n = h.shape[0]
    n_ec = N_EXPERTS // EXPERT_CHUNK
    gate_c = gate[:, :N_EXPERTS].reshape(n, n_ec, EXPERT_CHUNK).transpose(1, 0, 2)
    return pl.pallas_call(
        _moe_body,
        grid=(n // tm, n_ec),
        in_specs=[pl.BlockSpec((tm, D_MODEL), lambda i, e: (i, 0)),
                  pl.BlockSpec((1, tm, EXPERT_CHUNK), lambda i, e: (e, i, 0)),
                  pl.BlockSpec((EXPERT_CHUNK, D_MODEL, D_EXPERT), lambda i, e: (e, 0, 0)),
                  pl.BlockSpec((EXPERT_CHUNK, D_MODEL, D_EXPERT), lambda i, e: (e, 0, 0)),
                  pl.BlockSpec((EXPERT_CHUNK, D_EXPERT, D_MODEL), lambda i, e: (e, 0, 0)),
                  _full((1, D_MODEL)), _full((1, D_MODEL))],
        out_specs=pl.BlockSpec((tm, D_MODEL), lambda i, e: (i, 0)),
        out_shape=jax.ShapeDtypeStruct((n, D_MODEL), F32),
        scratch_shapes=[pltpu.VMEM((tm, D_MODEL), BF16), pltpu.VMEM((tm, D_MODEL), F32)],
        compiler_params=_params("parallel", "arbitrary"),
        name="moe",
    )(h, gate_c, pw["w1"], pw["w3"], pw["w2"], pw["ln2g"], pw["ln2b"])


def _prepare_weights(w_in, rw_mu, rw_w0, rw_w_up, rw_a0, rw_a_up, rw_g_up, rw_k_k, rw_k_a, rw_r_k,
                     rw_ln_g, rw_ln_b, idx_ln_g, idx_ln_b, w_out, ln1_g, ln1_b,
                     moe_w_grp, moe_w_exp, moe_w1, moe_w3, moe_w2, ln2_g, ln2_b):
    row = lambda z: z.reshape(1, -1).astype(F32)
    c5 = RW_COLS + 4 * D_ATT
    pad_to = lambda z, w: jnp.pad(z, ((0, 0), (0, w - z.shape[1])))
    wtail = jnp.concatenate([pad_to(w_in[:, c5:c5 + IDX_DIM], LANES),
                             pad_to(w_in[:, c5 + IDX_DIM:], LANES)], axis=1)
    zeros_w = jnp.zeros((LORA_W, D_RWKV), F32)
    return {
        "wrw": w_in[:, :RW_COLS].astype(BF16),
        "watt": w_in[:, RW_COLS:c5].astype(BF16),
        "wtail": wtail.astype(BF16),
        "mu": row(rw_mu), "w0": row(rw_w0), "a0": row(rw_a0), "kk": row(rw_k_k), "ka": row(rw_k_a),
        "wup": jnp.concatenate([rw_w_up, zeros_w], 0).astype(BF16),
        "aup": jnp.concatenate([zeros_w, rw_a_up], 0).astype(BF16),
        "gup": rw_g_up.astype(BF16),
        "lng": pad_to(row(idx_ln_g), LANES), "lnb": pad_to(row(idx_ln_b), LANES),
        "rk": row(rw_r_k), "rwlng": row(rw_ln_g), "rwlnb": row(rw_ln_b),
        "wout": w_out.astype(BF16), "ln1g": row(ln1_g), "ln1b": row(ln1_b),
        "wrouter": pad_to(jnp.concatenate([moe_w_exp, moe_w_grp], 1).astype(F32), LANES),
        "w1": moe_w1.astype(BF16), "w3": moe_w3.astype(BF16), "w2": moe_w2.astype(BF16),
        "ln2g": row(ln2_g), "ln2b": row(ln2_b),
    }


def _row_tile(n, want):
    return min(n, want)


def _layer(x, x_shift, pos_tables, pw, attend, scan, *, tm_proj, tm_post, tm_moe):
    n = x.shape[0]
    outs = _proj(x, x_shift, pos_tables, pw, _row_tile(n, tm_proj))
    rw = outs[:7]
    qb, kf, vf, kb, vt, qib, kif, kib, wi = outs[7:]
    o_rw, s_fin = scan(rw)
    o_att = attend(qb, kf, vf, kb, vt, qib, kif, kib, wi)
    h, gate = _post(x, o_rw, o_att, pw, _row_tile(n, tm_post))
    y = _moe(h, gate, pw, _row_tile(n, tm_moe))
    return y, kf, vf, kif, s_fin


def kernel(x_prompt, x_sample, cache_k, cache_v, cache_kidx, state_wkv, state_shift, page_table, w_in, rw_mu, rw_w0, rw_w_up, rw_a0, rw_a_up, rw_g_up, rw_k_k, rw_k_a, rw_r_k, rw_ln_g, rw_ln_b, idx_ln_g, idx_ln_b, w_out, ln1_g, ln1_b, moe_w_grp, moe_w_exp, moe_w1, moe_w3, moe_w2, ln2_g, ln2_b):
    pw = _prepare_weights(w_in, rw_mu, rw_w0, rw_w_up, rw_a0, rw_a_up, rw_g_up, rw_k_k, rw_k_a, rw_r_k,
                          rw_ln_g, rw_ln_b, idx_ln_g, idx_ln_b, w_out, ln1_g, ln1_b,
                          moe_w_grp, moe_w_exp, moe_w1, moe_w3, moe_w2, ln2_g, ln2_b)
    bp, tp, _ = x_prompt.shape
    bs, ts, _ = x_sample.shape
    assert ts == 1
    n_pages = page_table.shape[1]
    past = n_pages * PAGE_SIZE
    assert n_pages % SCORE_PAGES == 0 and n_pages % ATTEND_PAGES == 0

    xp = x_prompt.reshape(bp * tp, D_MODEL)

    def attend_p(qb, kf, vf, kb, vt, qib, kif, kib, wi):
        return _dsa_prompt(qib, wi.T, qb, kib, kb, vt, bp, tp)

    def scan_p(rw):
        s0 = jnp.zeros((bp, N_HEADS, HEAD_DIM, HEAD_DIM), F32)
        return _scan_chunked(rw, s0, pw["rk"], pw["rwlng"], pw["rwlnb"], bp, tp)

    y_p, k_p, v_p, ki_p, wkv_p = _layer(xp, None, _rope_tables(jnp.arange(tp)),
                                        pw, attend_p, scan_p, tm_proj=512, tm_post=512, tm_moe=1024)

    xs = x_sample.reshape(bs, D_MODEL)

    kidx_t = jnp.transpose(cache_kidx, (0, 2, 1))
    k_t = jnp.transpose(cache_k, (0, 2, 3, 1))
    v_t = jnp.transpose(cache_v, (0, 2, 3, 1))

    def attend_s(qb, kf, vf, kb, vt, qib, kif, kib, wi):
        scores, score_new = _sample_scores(page_table, qib, wi, kif.astype(cache_kidx.dtype), kidx_t, bs, n_pages)
        sel = _sample_select(scores.reshape(bs, past), score_new.reshape(bs, LANES), past)
        return _sample_attend(page_table, qb, sel, kf.astype(cache_k.dtype), vf.astype(cache_v.dtype),
                              k_t, v_t, bs, n_pages)

    def scan_s(rw):
        return _scan_step(rw, state_wkv.astype(F32), pw["rk"], pw["rwlng"], pw["rwlnb"], bs)

    y_s, k_s, v_s, ki_s, wkv_s = _layer(xs, state_shift.astype(x_sample.dtype),
                                        _rope_tables(jnp.full((bs,), past, I32)),
                                        pw, attend_s, scan_s, tm_proj=256, tm_post=512, tm_moe=1024)

    heads = lambda z, b, t: z.reshape(b, t, N_HEADS, HEAD_DIM)
    heads_t = lambda z: z.reshape(bp, N_HEADS, HEAD_DIM, tp).transpose(0, 3, 1, 2)
    return (y_p.reshape(bp, tp, D_MODEL), y_s.reshape(bs, ts, D_MODEL),
            heads_t(k_p).astype(cache_k.dtype), heads(v_p, bp, tp).astype(cache_v.dtype),
            ki_p.transpose(0, 2, 1).astype(cache_kidx.dtype),
            wkv_p.astype(state_wkv.dtype), x_prompt[:, -1].astype(state_shift.dtype),
            heads(k_s, bs, ts).astype(cache_k.dtype), heads(v_s, bs, ts).astype(cache_v.dtype),
            ki_s.reshape(bs, ts, IDX_DIM).astype(cache_kidx.dtype),
            wkv_s.astype(state_wkv.dtype), x_sample[:, -1].astype(state_shift.dtype))
```
